```python
import math
import jax, jax.numpy as jnp
from jax import lax
import numpy as np

D_MODEL = 1024
BATCH = 2
SEQ = 16384
DEPTH = 1
DEC_BATCH = 128
DEC_SEQ = 4
PAST_LEN = 8192
PAGE_SIZE = 128

ATTN_HEADS = 8
HEAD_DIM = 64
ATTN_DIM = ATTN_HEADS * HEAD_DIM
MOBA_BLOCK = 256
MOBA_TOPK = 3
Q_CHUNK = 128
ROPE_THETA = 10000.0
D_INNER = D_MODEL
SSM_HEAD_DIM = 64
SSM_HEADS = D_INNER // SSM_HEAD_DIM
SSM_GROUPS = 2
D_STATE = 128
CONV_W = 4
CONV_DIM = D_INNER + 2 * SSM_GROUPS * D_STATE
SSD_CHUNK = 128
N_BRANCH = 2
PLE_DIM = 256
IN_SIZES = (ATTN_DIM, ATTN_DIM, ATTN_DIM, ATTN_DIM, D_INNER, CONV_DIM, SSM_HEADS, N_BRANCH * D_MODEL)
IN_DIM = 4 * ATTN_DIM + D_INNER + CONV_DIM + SSM_HEADS + N_BRANCH * D_MODEL
EPS = 1e-6
NEG_INF = -1e30

kernel_name = 'moba_ssd_gated_hybrid_step'

F32 = jnp.float32


def rms_norm(x, w):
    xf = x.astype(F32)
    y = xf * lax.rsqrt(jnp.mean(xf * xf, axis=-1, keepdims=True) + EPS)
    return (y * w.astype(F32)).astype(x.dtype)


def gated_rms_norm(y, z, w):
    yz = y.astype(F32) * jax.nn.silu(z.astype(F32))
    shp = yz.shape
    yg = yz.reshape(shp[:-1] + (SSM_GROUPS, D_INNER // SSM_GROUPS))
    yg = yg * lax.rsqrt(jnp.mean(yg * yg, axis=-1, keepdims=True) + EPS)
    return yg.reshape(shp) * w.astype(F32)


def rope(x, pos):
    half = HEAD_DIM // 2
    inv = ROPE_THETA ** (-jnp.arange(half, dtype=F32) / half)
    ang = pos.astype(F32)[:, None] * inv[None, :]
    cos = jnp.cos(ang)[None, :, None, :]
    sin = jnp.sin(ang)[None, :, None, :]
    xf = x.astype(F32)
    x1, x2 = xf[..., :half], xf[..., half:]
    return jnp.concatenate([x1 * cos - x2 * sin, x2 * cos + x1 * sin], axis=-1).astype(x.dtype)


def moba_attention(q, k, v, q_start):
    bsz, t_q, n_h, hd = q.shape
    t_k = k.shape[1]
    n_blk = -(-t_k // MOBA_BLOCK)
    pad = n_blk * MOBA_BLOCK - t_k
    if pad:
        k = jnp.pad(k, ((0, 0), (0, pad), (0, 0), (0, 0)))
        v = jnp.pad(v, ((0, 0), (0, pad), (0, 0), (0, 0)))
    kb = k.reshape(bsz, n_blk, MOBA_BLOCK, n_h, hd)
    vb = v.reshape(bsz, n_blk, MOBA_BLOCK, n_h, hd)
    k_mean = jnp.mean(kb.astype(F32), axis=2)
    n_sel = min(MOBA_TOPK, n_blk)
    qc = Q_CHUNK if t_q % Q_CHUNK == 0 else t_q
    n_qc = t_q // qc
    q_chunks = q.reshape(bsz, n_qc, qc, n_h, hd).swapaxes(0, 1)
    b_ix = jnp.arange(bsz)[:, None, None, None]
    h_ix = jnp.arange(n_h)[None, :, None, None]
    blk_ids = jnp.arange(n_blk)
    scale = hd ** -0.5

    def one_chunk(args):
        qb, ci = args
        start = q_start + ci * qc
        pos = start + jnp.arange(qc)
        own = pos // MOBA_BLOCK
        own0 = start // MOBA_BLOCK
        gate = jnp.einsum('bqhd,bnhd->bhqn', qb.astype(F32), k_mean)
        gate = jnp.where(blk_ids[None, :] < own[:, None], gate, NEG_INF)
        _, sel = lax.top_k(gate, n_sel)
        sel_ok = sel < own[None, None, :, None]
        k_sel = kb[b_ix, sel, :, h_ix, :]
        v_sel = vb[b_ix, sel, :, h_ix, :]
        s_sel = jnp.einsum('bqhd,bhqkjd->bhqkj', qb, k_sel, preferred_element_type=F32)
        s_sel = jnp.where(sel_ok[..., None], s_sel, NEG_INF).reshape(bsz, n_h, qc, n_sel * MOBA_BLOCK)
        k_own = lax.dynamic_slice_in_dim(k, own0 * MOBA_BLOCK, MOBA_BLOCK, axis=1)
        v_own = lax.dynamic_slice_in_dim(v, own0 * MOBA_BLOCK, MOBA_BLOCK, axis=1)
        s_own = jnp.einsum('bqhd,bkhd->bhqk', qb, k_own, preferred_element_type=F32)
        own_pos = own0 * MOBA_BLOCK + jnp.arange(MOBA_BLOCK)
        s_own = jnp.where(own_pos[None, :] <= pos[:, None], s_own, NEG_INF)
        probs = jax.nn.softmax(jnp.concatenate([s_sel, s_own], axis=-1) * scale, axis=-1)
        p_sel = probs[..., :n_sel * MOBA_BLOCK].reshape(bsz, n_h, qc, n_sel, MOBA_BLOCK).astype(v.dtype)
        p_own = probs[..., n_sel * MOBA_BLOCK:].astype(v.dtype)
        return (jnp.einsum('bhqkj,bhqkjd->bqhd', p_sel, v_sel)
                + jnp.einsum('bhqk,bkhd->bqhd', p_own, v_own))

    out = lax.map(one_chunk, (q_chunks, jnp.arange(n_qc)))
    return out.swapaxes(0, 1).reshape(bsz, t_q, n_h, hd)


def causal_conv(xf, w, b, t):
    acc = b.astype(xf.dtype)
    for i in range(CONV_W):
        acc = acc + xf[:, i:i + t] * w[i]
    return acc


def ssd_scan(xs, dt, a, bm, cm, h0):
    bsz, t, n_h, p = xs.shape
    g, n = bm.shape[2], bm.shape[3]
    hpg = n_h // g
    cl = SSD_CHUNK if t % SSD_CHUNK == 0 else t
    nc = t // cl
    x = xs.reshape(bsz, nc, cl, g, hpg, p)
    dtc = dt.reshape(bsz, nc, cl, g, hpg)
    b = bm.reshape(bsz, nc, cl, g, n)
    c = cm.reshape(bsz, nc, cl, g, n)
    acs = jnp.cumsum(dtc * a.reshape(g, hpg), axis=2)
    diff = acs[:, :, :, None] - acs[:, :, None, :]
    causal = jnp.tril(jnp.ones((cl, cl), dtype=bool))[:, :, None, None]
    lmat = jnp.exp(jnp.where(causal, diff, -jnp.inf))
    cb = jnp.einsum('bclgn,bcsgn->bcgls', c, b)
    y_diag = jnp.einsum('bcgls,bclsgh,bcsgh,bcsghp->bclghp', cb, lmat, dtc, x)
    decay_to_end = jnp.exp(acs[:, :, -1:] - acs)
    chunk_states = jnp.einsum('bcsgn,bcsgh,bcsghp->bcghpn', b, decay_to_end * dtc, x)
    chunk_decay = jnp.exp(acs[:, :, -1])

    def step(h, inp):
        dec, st = inp
        return dec[..., None, None] * h + st, h

    h_final, h_prev = lax.scan(step, h0.reshape(bsz, g, hpg, p, n),
                               (chunk_decay.swapaxes(0, 1), chunk_states.swapaxes(0, 1)))
    h_prev = h_prev.swapaxes(0, 1)
    y_off = jnp.einsum('bclgn,bcghpn,bclgh->bclghp', c, h_prev, jnp.exp(acs))
    y = (y_diag + y_off).reshape(bsz, t, n_h, p)
    return y, h_final.reshape(bsz, n_h, p, n)


def decoder_layer(h, p_l, q_start, k_past, v_past, conv_state, ssm_state,
                  norm_mix_w, w_in, conv_w, conv_b, dt_bias, a_log, d_skip, ssm_norm_w,
                  w_attn_br, w_ssm_br, w_out, ple_norm_w, w_ple_gate, w_ple):
    bsz, t, _ = h.shape
    u = rms_norm(h, norm_mix_w)
    proj = u @ w_in
    offsets = np.cumsum(IN_SIZES)[:-1].tolist()
    q, k, v, z_a, z_m, xbc, dt_raw, gate_logits = jnp.split(proj, offsets, axis=-1)
    pos = q_start + jnp.arange(t)
    q = rope(q.reshape(bsz, t, ATTN_HEADS, HEAD_DIM), pos)
    k = rope(k.reshape(bsz, t, ATTN_HEADS, HEAD_DIM), pos)
    v = v.reshape(bsz, t, ATTN_HEADS, HEAD_DIM)
    if k_past is None:
        k_all, v_all = k, v
    else:
        k_all = jnp.concatenate([k_past.astype(k.dtype), k], axis=1)
        v_all = jnp.concatenate([v_past.astype(v.dtype), v], axis=1)
    o = moba_attention(q, k_all, v_all, q_start).reshape(bsz, t, ATTN_DIM)
    y_attn = (o * jax.nn.silu(z_a)) @ w_attn_br
    xbc_full = jnp.concatenate([conv_state.astype(xbc.dtype), xbc], axis=1)
    conv_new = xbc_full[:, -(CONV_W - 1):]
    xbc_c = jax.nn.silu(causal_conv(xbc_full, conv_w, conv_b, t))
    xs, bm, cm = jnp.split(xbc_c, [D_INNER, D_INNER + SSM_GROUPS * D_STATE], axis=-1)
    dt = jax.nn.softplus(dt_raw.astype(F32) + dt_bias.astype(F32))
    a = -jnp.exp(a_log.astype(F32))
    xs_h = xs.reshape(bsz, t, SSM_HEADS, SSM_HEAD_DIM).astype(F32)
    y_ssd, ssm_new = ssd_scan(xs_h, dt, a,
                              bm.reshape(bsz, t, SSM_GROUPS, D_STATE).astype(F32),
                              cm.reshape(bsz, t, SSM_GROUPS, D_STATE).astype(F32),
                              ssm_state.astype(F32))
    y_ssd = y_ssd + d_skip.astype(F32)[:, None] * xs_h
    y_ssm = gated_rms_norm(y_ssd.reshape(bsz, t, D_INNER), z_m, ssm_norm_w).astype(h.dtype) @ w_ssm_br
    g_attn, g_ssm = jnp.split(jax.nn.sigmoid(gate_logits), N_BRANCH, axis=-1)
    h = h + (g_attn * y_attn + g_ssm * y_ssm) @ w_out
    ple_gate = jax.nn.sigmoid(rms_norm(h, ple_norm_w) @ w_ple_gate)
    h = h + (p_l.astype(h.dtype) @ w_ple) * ple_gate
    return h, k, v, conv_new, ssm_new.astype(ssm_state.dtype)


def setup_inputs(seed: int = 0) -> dict:
    key = jax.random.key(seed)
    ks = jax.random.split(key, 26)
    n_pages = PAST_LEN // PAGE_SIZE
    n_used = DEC_BATCH * n_pages
    n_phys = n_used + n_used // 4
    nrm = lambda k, s, sc: jax.random.normal(k, s, F32) * sc
    dt0 = jnp.exp(jax.random.uniform(ks[12], (DEPTH, SSM_HEADS), F32, math.log(1e-3), math.log(1e-1)))
    page_table = jax.random.permutation(ks[6], n_phys)[:n_used].reshape(DEC_BATCH, n_pages).astype(jnp.int32)
    return {
        'x_prompt': nrm(ks[0], (BATCH, SEQ, D_MODEL), 1.0),
        'x_sample': nrm(ks[1], (DEC_BATCH, DEC_SEQ, D_MODEL), 1.0),
        'cache_k': nrm(ks[2], (DEPTH, n_phys, PAGE_SIZE, ATTN_HEADS, HEAD_DIM), 1.0),
        'cache_v': nrm(ks[3], (DEPTH, n_phys, PAGE_SIZE, ATTN_HEADS, HEAD_DIM), 1.0),
        'state_conv': nrm(ks[4], (DEPTH, DEC_BATCH, CONV_W - 1, CONV_DIM), 1.0),
        'state_ssm': nrm(ks[5], (DEPTH, DEC_BATCH, SSM_HEADS, SSM_HEAD_DIM, D_STATE), 0.5),
        'page_table': page_table,
        'p_prompt': nrm(ks[7], (DEPTH, BATCH, SEQ, PLE_DIM), 1.0),
        'p_sample': nrm(ks[8], (DEPTH, DEC_BATCH, DEC_SEQ, PLE_DIM), 1.0),
        'norm_mix_w': 1.0 + nrm(ks[9], (DEPTH, D_MODEL), 0.05),
        'w_in': nrm(ks[10], (DEPTH, D_MODEL, IN_DIM), D_MODEL ** -0.5),
        'conv_w': nrm(ks[11], (DEPTH, CONV_W, CONV_DIM), 0.5 * CONV_W ** -0.5),
        'conv_b': nrm(ks[13], (DEPTH, CONV_DIM), 0.01),
        'dt_bias': dt0 + jnp.log(-jnp.expm1(-dt0)),
        'a_log': jnp.log(jax.random.uniform(ks[14], (DEPTH, SSM_HEADS), F32, 1.0, 16.0)),
        'd_skip': 1.0 + nrm(ks[15], (DEPTH, SSM_HEADS), 0.1),
        'ssm_norm_w': 1.0 + nrm(ks[16], (DEPTH, D_INNER), 0.05),
        'w_attn_br': nrm(ks[17], (DEPTH, ATTN_DIM, D_MODEL), ATTN_DIM ** -0.5),
        'w_ssm_br': nrm(ks[18], (DEPTH, D_INNER, D_MODEL), D_INNER ** -0.5),
        'w_out': nrm(ks[19], (DEPTH, D_MODEL, D_MODEL), D_MODEL ** -0.5),
        'ple_norm_w': 1.0 + nrm(ks[20], (DEPTH, D_MODEL), 0.05),
        'w_ple_gate': nrm(ks[21], (DEPTH, D_MODEL, D_MODEL), D_MODEL ** -0.5),
        'w_ple': nrm(ks[22], (DEPTH, PLE_DIM, D_MODEL), PLE_DIM ** -0.5),
        'final_norm_w': 1.0 + nrm(ks[23], (D_MODEL,), 0.05),
    }


def reference(x_prompt, x_sample, cache_k, cache_v, state_conv, state_ssm, page_table, p_prompt, p_sample,
              norm_mix_w, w_in, conv_w, conv_b, dt_bias, a_log, d_skip, ssm_norm_w,
              w_attn_br, w_ssm_br, w_out, ple_norm_w, w_ple_gate, w_ple, final_norm_w):
    n_seq, n_pages = page_table.shape
    past_len = n_pages * cache_k.shape[2]
    h_p, h_s = x_prompt, x_sample
    kp_l, vp_l, cp_l, sp_l, ks_l, vs_l, cs_l, ss_l = [], [], [], [], [], [], [], []
    for l in range(DEPTH):
        lw = (norm_mix_w[l], w_in[l], conv_w[l], conv_b[l], dt_bias[l], a_log[l], d_skip[l], ssm_norm_w[l],
              w_attn_br[l], w_ssm_br[l], w_out[l], ple_norm_w[l], w_ple_gate[l], w_ple[l])
        bp = h_p.shape[0]
        conv0 = jnp.zeros((bp, CONV_W - 1, CONV_DIM), h_p.dtype)
        ssm0 = jnp.zeros((bp, SSM_HEADS, SSM_HEAD_DIM, D_STATE), h_p.dtype)
        h_p, kp, vp, cp, sp = decoder_layer(h_p, p_prompt[l], 0, None, None, conv0, ssm0, *lw)
        k_past = cache_k[l][page_table].reshape(n_seq, past_len, ATTN_HEADS, HEAD_DIM)
        v_past = cache_v[l][page_table].reshape(n_seq, past_len, ATTN_HEADS, HEAD_DIM)
        h_s, ks_, vs_, cs_, ss_ = decoder_layer(h_s, p_sample[l], past_len, k_past, v_past,
                                                state_conv[l], state_ssm[l], *lw)
        kp_l.append(kp); vp_l.append(vp); cp_l.append(cp); sp_l.append(sp)
        ks_l.append(ks_); vs_l.append(vs_); cs_l.append(cs_); ss_l.append(ss_)
    y_prompt = rms_norm(h_p, final_norm_w)
    y_sample = rms_norm(h_s, final_norm_w)
    return (y_prompt, y_sample,
            jnp.stack(kp_l), jnp.stack(vp_l), jnp.stack(cp_l), jnp.stack(sp_l),
            jnp.stack(ks_l), jnp.stack(vs_l), jnp.stack(cs_l), jnp.stack(ss_l))
```

```python
import functools

import jax
import jax.numpy as jnp
from jax import lax
from jax.experimental import pallas as pl
from jax.experimental.pallas import tpu as pltpu

F32 = jnp.float32
BF16 = jnp.bfloat16

HEAD_DIM = 64
MOBA_BLOCK = 256
MOBA_TOPK = 3
ROPE_THETA = 10000.0
SSM_HEAD_DIM = 64
SSM_GROUPS = 2
D_STATE = 128
CONV_W = 4
SSD_CHUNK = 128
EPS = 1e-6
NEG_INF = -1e30

LANES = 128
SUBLANES = 8
VMEM_LIMIT_BYTES = 56 * 1024 * 1024


def _sigmoid(x):
    return 1.0 / (1.0 + jnp.exp(-x))


def _silu(x):
    return x * _sigmoid(x)


def _softplus(x):
    return jnp.maximum(x, 0.0) + jnp.log1p(jnp.exp(-jnp.abs(x)))


def _rms(x, w):
    ms = jnp.mean(x * x, axis=-1, keepdims=True)
    return x * lax.rsqrt(ms + EPS) * w


def _dot(a, b):
    return jnp.dot(a, b, preferred_element_type=F32)


def _dot_nt(a, b):
    return lax.dot_general(a, b, (((1,), (1,)), ((), ())), preferred_element_type=F32)


def _split2(x):
    hi = x.astype(BF16)
    lo = (x - hi.astype(F32)).astype(BF16)
    return hi, lo


def _split3(x):
    b1 = x.astype(BF16)
    r1 = x - b1.astype(F32)
    b2 = r1.astype(BF16)
    b3 = (r1 - b2.astype(F32)).astype(BF16)
    return b1, b2, b3


def _params(sem):
    return pltpu.CompilerParams(dimension_semantics=sem, vmem_limit_bytes=VMEM_LIMIT_BYTES)


def _rope_table_kernel(inv_ref, cos_ref, sin_ref, *, start, rows):
    i = pl.program_id(0)
    pos = lax.broadcasted_iota(jnp.int32, (rows, LANES), 0) + (start + i * rows)
    lane = lax.broadcasted_iota(jnp.int32, (rows, LANES), 1)
    ang = pos.astype(F32) * inv_ref[...]
    first_half = (lane % HEAD_DIM) < (HEAD_DIM // 2)
    cos_ref[...] = jnp.cos(ang)
    s = jnp.sin(ang)
    sin_ref[...] = jnp.where(first_half, -s, s)


def _rope_tables(t, start):
    half = HEAD_DIM // 2
    inv = ROPE_THETA ** (-jnp.arange(half, dtype=F32) / half)
    inv = jnp.tile(inv, LANES // half)[None, :]
    rows = min(t, 512)
    assert t % rows == 0
    spec = pl.BlockSpec((rows, LANES), lambda i: (i, 0))
    return pl.pallas_call(
        functools.partial(_rope_table_kernel, start=start, rows=rows),
        grid=(t // rows,),
        in_specs=[pl.BlockSpec((1, LANES), lambda i: (0, 0))],
        out_specs=[spec, spec],
        out_shape=[jax.ShapeDtypeStruct((t, LANES), F32)] * 2,
        compiler_params=_params(("parallel",)),
        name="rope_tables",
    )(inv)


def _rope(x, cos, sin_signed):
    lane = lax.broadcasted_iota(jnp.int32, cos.shape, 1)
    first_half = (lane % HEAD_DIM) < (HEAD_DIM // 2)
    half = HEAD_DIM // 2
    outs = []
    for c in range(x.shape[1] // LANES):
        xs = x[:, c * LANES:(c + 1) * LANES]
        partner = jnp.where(first_half, pltpu.roll(xs, LANES - half, 1), pltpu.roll(xs, half, 1))
        outs.append(xs * cos + partner * sin_signed)
    return jnp.concatenate(outs, axis=1)


def _inproj_kernel(x_ref, nw_ref, w_ref, cos_ref, sin_ref, *out_refs, widths, with_kmean):
    u = _rms(x_ref[...], nw_ref[...]).astype(BF16)
    cos = cos_ref[...]
    sin = sin_ref[...]
    off = 0
    for idx, wd in enumerate(widths):
        r = _dot(u, w_ref[:, off:off + wd])
        off += wd
        if idx < 2:
            r = _rope(r, cos, sin)
        out_refs[idx][...] = r
        if idx == 1 and with_kmean:
            out_refs[len(widths)][...] = jnp.mean(r, axis=0, keepdims=True)


def _inproj(x2d, nw, w_pad, cos, sin, widths, tm, cos_map, with_kmean):
    n, d = x2d.shape
    assert n % tm == 0
    out_shape = [jax.ShapeDtypeStruct((n, wd), F32) for wd in widths]
    out_specs = [pl.BlockSpec((tm, wd), lambda i: (i, 0)) for wd in widths]
    if with_kmean:
        out_shape.append(jax.ShapeDtypeStruct((n // tm, 1, widths[1]), F32))
        out_specs.append(pl.BlockSpec((None, 1, widths[1]), lambda i: (i, 0, 0)))
    return pl.pallas_call(
        functools.partial(_inproj_kernel, widths=widths, with_kmean=with_kmean),
        grid=(n // tm,),
        in_specs=[
            pl.BlockSpec((tm, d), lambda i: (i, 0)),
            pl.BlockSpec((1, d), lambda i: (0, 0)),
            pl.BlockSpec(w_pad.shape, lambda i: (0, 0)),
            pl.BlockSpec((tm, LANES), cos_map),
            pl.BlockSpec((tm, LANES), cos_map),
        ],
        out_specs=out_specs,
        out_shape=out_shape,
        compiler_params=_params(("parallel",)),
        name="inproj",
    )(x2d, nw, w_pad, cos, sin)


def _top3_rows(gate, valid, blk):
    g = jnp.where(valid, gate, NEG_INF)
    sel = jnp.zeros(g.shape, F32)
    big = jnp.float32(1e9)
    for _ in range(MOBA_TOPK):
        mx = jnp.max(g, axis=0, keepdims=True)
        idx = jnp.min(jnp.where(g == mx, blk, big), axis=0, keepdims=True)
        pick = blk == idx
        sel = jnp.where(pick, 1.0, sel)
        g = jnp.where(pick, -jnp.inf, g)
    return jnp.where(valid, sel, 0.0)


def _moba_prompt_kernel(q_ref, k_ref, vt_ref, km_ref, o_ref, sel_sc, qs_sc, acc_sc, m_sc, l_sc, *, nblk):
    tq = MOBA_BLOCK
    own = pl.program_id(2)
    q = q_ref[...]
    km_hi, km_lo = _split2(km_ref[...])
    lane = lax.broadcasted_iota(jnp.int32, (1, LANES), 1)
    blk = lax.broadcasted_iota(jnp.int32, (nblk, tq), 0).astype(F32)
    kpos = lax.broadcasted_iota(jnp.int32, (tq, tq), 0)
    qpos = lax.broadcasted_iota(jnp.int32, (tq, tq), 1)
    causal = kpos <= qpos
    k_own = k_ref[own]
    vt_own = vt_ref[own]
    for hh in range(2):
        qh = jnp.where((lane // HEAD_DIM) == hh, q, 0.0)
        q_hi, q_lo = _split2(qh)
        gate = _dot_nt(km_hi, q_hi) + _dot_nt(km_hi, q_lo) + _dot_nt(km_lo, q_hi)
        sel_sc[hh] = _top3_rows(gate, blk < own.astype(F32), blk)
        qs = (qh * (HEAD_DIM ** -0.5)).astype(BF16)
        qs_sc[hh] = qs
        s = jnp.where(causal, _dot_nt(k_own, qs), NEG_INF)
        m = jnp.max(s, axis=0, keepdims=True)
        p = jnp.exp(s - m)
        m_sc[hh] = m
        l_sc[hh] = jnp.sum(p, axis=0, keepdims=True)
        acc_sc[hh] = _dot(vt_own, p.astype(BF16))

    def body(j, carry):
        kj = k_ref[j]
        vtj = vt_ref[j]
        for hh in range(2):
            s = _dot_nt(kj, qs_sc[hh])
            s = jnp.where(sel_sc[hh, pl.ds(j, 1), :] > 0.0, s, NEG_INF)
            m_old = m_sc[hh]
            m_new = jnp.maximum(m_old, jnp.max(s, axis=0, keepdims=True))
            alpha = jnp.exp(m_old - m_new)
            p = jnp.exp(s - m_new)
            l_sc[hh] = alpha * l_sc[hh] + jnp.sum(p, axis=0, keepdims=True)
            acc_sc[hh] = alpha * acc_sc[hh] + _dot(vtj, p.astype(BF16))
            m_sc[hh] = m_new
        return carry

    lax.fori_loop(0, own, body, 0)
    o_ref[0:HEAD_DIM, :] = acc_sc[0][0:HEAD_DIM, :] / l_sc[0]
    o_ref[HEAD_DIM:LANES, :] = acc_sc[1][HEAD_DIM:LANES, :] / l_sc[1]


def _moba_prompt(q, k, v, kmean):
    bsz, t, a = q.shape
    assert t % MOBA_BLOCK == 0 and a % LANES == 0
    nblk = t // MOBA_BLOCK
    npair = a // LANES
    k_blk = k.astype(BF16).reshape(bsz, nblk, MOBA_BLOCK, a)
    vt_blk = v.astype(BF16).reshape(bsz, nblk, MOBA_BLOCK, npair, LANES).transpose(0, 3, 1, 4, 2)
    o_t = pl.pallas_call(
        functools.partial(_moba_prompt_kernel, nblk=nblk),
        grid=(bsz, npair, nblk),
        in_specs=[
            pl.BlockSpec((None, MOBA_BLOCK, LANES), lambda b, hp, i: (b, i, hp)),
            pl.BlockSpec((None, nblk, MOBA_BLOCK, LANES), lambda b, hp, i: (b, 0, 0, hp)),
            pl.BlockSpec((None, None, nblk, LANES, MOBA_BLOCK), lambda b, hp, i: (b, hp, 0, 0, 0)),
            pl.BlockSpec((None, nblk, LANES), lambda b, hp, i: (b, 0, hp)),
        ],
        out_specs=pl.BlockSpec((None, LANES, MOBA_BLOCK), lambda b, hp, i: (b, hp, i)),
        out_shape=jax.ShapeDtypeStruct((bsz, a, t), F32),
        scratch_shapes=[
            pltpu.VMEM((2, nblk, MOBA_BLOCK), F32),
            pltpu.VMEM((2, MOBA_BLOCK, LANES), BF16),
            pltpu.VMEM((2, LANES, MOBA_BLOCK), F32),
            pltpu.VMEM((2, 1, MOBA_BLOCK), F32),
            pltpu.VMEM((2, 1, MOBA_BLOCK), F32),
        ],
        compiler_params=_params(("parallel", "parallel", "arbitrary")),
        name="moba_prompt",
    )(q, k_blk, vt_blk, kmean)
    return o_t.transpose(0, 2, 1)


def _moba_sample_kernel(pt_ref, qrep_ref, kn_ref, vn_ref, *refs, pps, nb, tq, nheads):
    del pt_ref
    k_pages = refs[:pps]
    v_pages = refs[pps:2 * pps]
    o_ref = refs[2 * pps]
    g_sc, m_sc, l_sc, o_sc = refs[2 * pps + 1:]
    c = pl.program_id(1)
    rows = nheads * tq
    a = nheads * HEAD_DIM
    row_h = lax.broadcasted_iota(jnp.int32, (rows, a), 0) // tq
    lane_h = lax.broadcasted_iota(jnp.int32, (rows, a), 1) // HEAD_DIM
    qx = jnp.where(row_h == lane_h, qrep_ref[...], 0.0)
    qs = (qx * (HEAD_DIM ** -0.5)).astype(BF16)
    per_blk = MOBA_BLOCK // k_pages[0].shape[0]
    for jj in range(pps // per_blk):
        kj = jnp.concatenate([k_pages[jj * per_blk + r][...] for r in range(per_blk)], axis=0)
        vj = jnp.concatenate([v_pages[jj * per_blk + r][...] for r in range(per_blk)], axis=0)
        km = jnp.mean(kj, axis=0, keepdims=True)
        blk = c * (pps // per_blk) + jj
        g_sc[blk] = jnp.sum(qx * km, axis=1, keepdims=True)
        s = _dot_nt(qs, kj.astype(BF16))
        m = jnp.max(s, axis=1, keepdims=True)
        p = jnp.exp(s - m)
        m_sc[blk] = m
        l_sc[blk] = jnp.sum(p, axis=1, keepdims=True)
        o_sc[blk] = _dot(p.astype(BF16), vj.astype(BF16))

    @pl.when(c == pl.num_programs(1) - 1)
    def _():
        gates = g_sc[...]
        blk3 = lax.broadcasted_iota(jnp.int32, gates.shape, 0).astype(F32)
        sel = _top3_rows(gates, blk3 >= 0.0, blk3) > 0.0
        m_all = m_sc[...]
        kn = kn_ref[...]
        vn = vn_ref[...]
        t_idx = lax.broadcasted_iota(jnp.int32, (rows, 1), 0) % tq
        s_own = []
        m_own = jnp.full((rows, 1), NEG_INF, F32)
        for kk in range(tq):
            sk = jnp.sum(qx * (HEAD_DIM ** -0.5) * kn[kk:kk + 1, :], axis=1, keepdims=True)
            sk = jnp.where(t_idx >= kk, sk, NEG_INF)
            s_own.append(sk)
            m_own = jnp.maximum(m_own, sk)
        m_fin = jnp.maximum(m_own, jnp.max(jnp.where(sel, m_all, NEG_INF), axis=0))
        w = jnp.exp(jnp.where(sel, m_all - m_fin, NEG_INF))
        l_fin = jnp.sum(w * l_sc[...], axis=0)
        o_fin = jnp.sum(w * o_sc[...], axis=0)
        for kk in range(tq):
            pk = jnp.exp(s_own[kk] - m_fin)
            l_fin = l_fin + pk
            o_fin = o_fin + pk * vn[kk:kk + 1, :]
        o_ref[...] = o_fin / l_fin


def _moba_sample(q, k_new, v_new, cache_k, cache_v, page_table):
    bs, tq, a = q.shape
    n_phys, page = cache_k.shape[0], cache_k.shape[1]
    n_pages = page_table.shape[1]
    past = n_pages * page
    nheads = a // HEAD_DIM
    assert MOBA_BLOCK % page == 0 and past % MOBA_BLOCK == 0 and tq <= MOBA_BLOCK
    per_blk = MOBA_BLOCK // page
    nb = past // MOBA_BLOCK
    assert nb >= MOBA_TOPK
    pps = per_blk * 4
    while n_pages % pps:
        pps -= per_blk
    ck = cache_k.reshape(n_phys, page, a)
    cv = cache_v.reshape(n_phys, page, a)
    rows = nheads * tq
    qrep = jnp.tile(q, (1, nheads, 1))

    def page_spec(r):
        return pl.BlockSpec((None, page, a), lambda b, c, pt: (pt[b, c * pps + r], 0, 0))

    seq_spec = pl.BlockSpec((None, tq, a), lambda b, c, pt: (b, 0, 0))
    row_spec = pl.BlockSpec((None, rows, a), lambda b, c, pt: (b, 0, 0))
    o_full = pl.pallas_call(
        functools.partial(_moba_sample_kernel, pps=pps, nb=nb, tq=tq, nheads=nheads),
        grid_spec=pltpu.PrefetchScalarGridSpec(
            num_scalar_prefetch=1,
            grid=(bs, n_pages // pps),
            in_specs=[row_spec, seq_spec, seq_spec] + [page_spec(r) for r in range(pps)] * 2,
            out_specs=row_spec,
            scratch_shapes=[
                pltpu.VMEM((nb, rows, 1), F32),
                pltpu.VMEM((nb, rows, 1), F32),
                pltpu.VMEM((nb, rows, 1), F32),
                pltpu.VMEM((nb, rows, a), F32),
            ],
        ),
        out_shape=jax.ShapeDtypeStruct((bs, rows, a), F32),
        compiler_params=_params(("parallel", "arbitrary")),
        name="moba_sample",
    )(page_table, qrep, k_new, v_new, *([ck] * pps), *([cv] * pps))
    o5 = o_full.reshape(bs, nheads, tq, nheads, HEAD_DIM)
    hidx = jnp.arange(nheads)
    return o5[:, hidx, :, hidx, :].transpose(1, 2, 0, 3).reshape(bs, tq, a)


def _ssd_prompt_kernel(xbc_ref, dt_ref, cw_ref, cb_ref, dtb_ref, alog_ref, dsk_ref, y_ref, hout_ref,
                       xwin_sc, h_sc, *, d_inner, nheads):
    cl = SSD_CHUNK
    c = pl.program_id(1)

    @pl.when(c == 0)
    def _():
        xwin_sc[0:SUBLANES, :] = jnp.zeros((SUBLANES, xwin_sc.shape[1]), F32)
        h_sc[...] = jnp.zeros(h_sc.shape, F32)

    cur = xbc_ref[...]
    xwin_sc[SUBLANES:SUBLANES + cl, :] = cur
    acc = jnp.broadcast_to(cb_ref[...], cur.shape)
    for i in range(CONV_W):
        acc = acc + xwin_sc[pl.ds(SUBLANES - (CONV_W - 1) + i, cl), :] * cw_ref[i:i + 1, :]
    xwin_sc[0:SUBLANES, :] = cur[cl - SUBLANES:cl, :]
    xc = _silu(acc)
    gn = SSM_GROUPS * D_STATE
    xs = xc[:, :d_inner]
    bm = xc[:, d_inner:d_inner + gn]
    cm = xc[:, d_inner + gn:d_inner + 2 * gn]

    dt = _softplus(dt_ref[...] + dtb_ref[...])
    da = dt * (-jnp.exp(alog_ref[...]))
    li = lax.broadcasted_iota(jnp.int32, (cl, cl), 0)
    si = lax.broadcasted_iota(jnp.int32, (cl, cl), 1)
    causal = li >= si
    tril = jnp.where(causal, 1.0, 0.0).astype(BF16)
    d1, d2, d3 = _split3(da)
    acs = _dot(tril, d1) + _dot(tril, d2) + _dot(tril, d3)
    acs_t = acs.T
    dt_t = dt.T
    xs_t = xs.T
    lane = lax.broadcasted_iota(jnp.int32, (1, LANES), 1)
    lo_half = lane < SSM_HEAD_DIM
    hpg = nheads // SSM_GROUPS
    ys = []
    for hp in range(nheads // 2):
        g = (2 * hp) // hpg
        bg = bm[:, g * D_STATE:(g + 1) * D_STATE]
        cg = cm[:, g * D_STATE:(g + 1) * D_STATE].astype(BF16)
        cb = _dot_nt(cg, bg.astype(BF16))
        x2 = xs[:, hp * LANES:(hp + 1) * LANES]
        x2b = x2.astype(BF16)
        yd = []
        ecol = []
        for hh in range(2):
            h = 2 * hp + hh
            acol = acs[:, h:h + 1]
            diff = acol - acs_t[h:h + 1, :]
            lm = jnp.exp(jnp.where(causal, diff, NEG_INF))
            mh = cb * lm * dt_t[h:h + 1, :]
            yd.append(_dot(mh.astype(BF16), x2b))
            ecol.append(jnp.exp(acol))
            alast = acs[cl - 1:cl, h:h + 1]
            wcol = jnp.exp(alast - acol) * dt[:, h:h + 1]
            st = _dot(xs_t[h * SSM_HEAD_DIM:(h + 1) * SSM_HEAD_DIM, :].astype(BF16), (bg * wcol).astype(BF16))
            hprev = h_sc[h]
            if hh == 0:
                hprev0 = hprev
            else:
                hprev2 = jnp.concatenate([hprev0, hprev], axis=0).astype(BF16)
            h_sc[h] = jnp.exp(alast) * hprev + st
        yo = _dot_nt(cg, hprev2) * jnp.where(lo_half, ecol[0], ecol[1])
        dsk2 = jnp.where(lo_half, dsk_ref[:, 2 * hp:2 * hp + 1], dsk_ref[:, 2 * hp + 1:2 * hp + 2])
        ys.append(jnp.where(lo_half, yd[0], yd[1]) + yo + dsk2 * x2)
    y_ref[...] = jnp.concatenate(ys, axis=1)

    @pl.when(c == pl.num_programs(1) - 1)
    def _():
        hout_ref[...] = h_sc[...]


def _ssd_prompt(xbc, dtp, cw, cb, dtb, alog, dsk, d_inner, nheads):
    bsz, t, cdim = xbc.shape
    assert t % SSD_CHUNK == 0 and nheads % 2 == 0 and (nheads // SSM_GROUPS) % 2 == 0
    nc = t // SSD_CHUNK
    small = lambda shape: pl.BlockSpec(shape, lambda b, c: (0, 0))
    return pl.pallas_call(
        functools.partial(_ssd_prompt_kernel, d_inner=d_inner, nheads=nheads),
        grid=(bsz, nc),
        in_specs=[
            pl.BlockSpec((None, SSD_CHUNK, cdim), lambda b, c: (b, c, 0)),
            pl.BlockSpec((None, SSD_CHUNK, LANES), lambda b, c: (b, c, 0)),
            small(cw.shape), small(cb.shape), small(dtb.shape), small(alog.shape), small(dsk.shape),
        ],
        out_specs=[
            pl.BlockSpec((None, SSD_CHUNK, d_inner), lambda b, c: (b, c, 0)),
            pl.BlockSpec((None, nheads, SSM_HEAD_DIM, D_STATE), lambda b, c: (b, 0, 0, 0)),
        ],
        out_shape=[
            jax.ShapeDtypeStruct((bsz, t, d_inner), F32),
            jax.ShapeDtypeStruct((bsz, nheads, SSM_HEAD_DIM, D_STATE), F32),
        ],
        scratch_shapes=[
            pltpu.VMEM((SUBLANES + SSD_CHUNK, cdim), F32),
            pltpu.VMEM((nheads, SSM_HEAD_DIM, D_STATE), F32),
        ],
        compiler_params=_params(("parallel", "arbitrary")),
        name="ssd_prompt",
    )(xbc, dtp, cw, cb, dtb, alog, dsk)


def _ssd_sample_kernel(xf_ref, dtx_ref, h0_ref, cw_ref, cb_ref, dtbx_ref, alogx_ref, dskx_ref, y_ref, hout_ref,
                       rows_sc, cpad_sc, bpad_sc, *, t, d_inner, nheads):
    cdim = xf_ref.shape[1]
    acc = jnp.broadcast_to(cb_ref[...], (t, cdim))
    for i in range(CONV_W):
        acc = acc + xf_ref[pl.ds(i, t), :] * cw_ref[i:i + 1, :]
    xc = _silu(acc)
    gn = SSM_GROUPS * D_STATE
    xs = xc[:, :d_inner]
    bm = xc[:, d_inner:d_inner + gn]
    cm = xc[:, d_inner + gn:d_inner + 2 * gn]
    dt = _softplus(dtx_ref[...] + dtbx_ref[...])
    da = dt * (-jnp.exp(alogx_ref[...]))
    acs = [da[0:1, :]]
    for s in range(1, t):
        acs.append(acs[-1] + da[s:s + 1, :])
    lane = lax.broadcasted_iota(jnp.int32, (1, d_inner), 1)
    grp = lane // (d_inner // SSM_GROUPS)

    rows_sc[...] = jnp.zeros(rows_sc.shape, F32)
    for s in range(t):
        rows_sc[s:s + 1, :] = jnp.exp(acs[t - 1] - acs[s]) * dt[s:s + 1, :] * xs[s:s + 1, :]
    rows_sc[t:t + 1, :] = jnp.exp(acs[t - 1])
    cols = rows_sc[...].T
    cpad_sc[...] = jnp.zeros(cpad_sc.shape, F32)
    bpad_sc[...] = jnp.zeros(bpad_sc.shape, F32)
    for g in range(SSM_GROUPS):
        cpad_sc[g, 0:t, :] = cm[:, g * D_STATE:(g + 1) * D_STATE]
        bpad_sc[g, 0:t, :] = bm[:, g * D_STATE:(g + 1) * D_STATE]

    h0 = h0_ref[...].reshape(d_inner, D_STATE)
    half = d_inner // SSM_GROUPS
    yoff_t = []
    hnew = []
    for g in range(SSM_GROUPS):
        h0g = h0[g * half:(g + 1) * half, :]
        yoff_t.append(_dot_nt(h0g.astype(BF16), cpad_sc[g].astype(BF16)))
        colg = cols[g * half:(g + 1) * half, :]
        w_hi, w_lo = _split2(jnp.where(lax.broadcasted_iota(jnp.int32, colg.shape, 1) < t, colg, 0.0))
        b_hi, b_lo = _split2(bpad_sc[g])
        upd = _dot(w_hi, b_hi) + _dot(w_hi, b_lo) + _dot(w_lo, b_hi)
        hnew.append(h0g * colg[:, t:t + 1] + upd)
    hout_ref[...] = jnp.concatenate(hnew, axis=0).reshape(hout_ref.shape)
    yoff = jnp.concatenate(yoff_t, axis=0).T

    for tt in range(t):
        y = yoff[tt:tt + 1, :] * jnp.exp(acs[tt]) + dskx_ref[...] * xs[tt:tt + 1, :]
        for s in range(tt + 1):
            cbs = []
            for g in range(SSM_GROUPS):
                sl = slice(g * D_STATE, (g + 1) * D_STATE)
                cbs.append(jnp.sum(cm[tt:tt + 1, sl] * bm[s:s + 1, sl], axis=1, keepdims=True))
            cbx = jnp.where(grp == 0, cbs[0], cbs[1])
            y = y + cbx * jnp.exp(acs[tt] - acs[s]) * dt[s:s + 1, :] * xs[s:s + 1, :]
        y_ref[tt:tt + 1, :] = y


def _ssd_sample(xf, dtx, h0, cw, cb, dtbx, alogx, dskx, t, d_inner, nheads):
    bs, tf, cdim = xf.shape
    assert SSM_GROUPS == 2 and t + 1 <= LANES
    small = lambda shape: pl.BlockSpec(shape, lambda b: (0, 0))
    return pl.pallas_call(
        functools.partial(_ssd_sample_kernel, t=t, d_inner=d_inner, nheads=nheads),
        grid=(bs,),
        in_specs=[
            pl.BlockSpec((None, tf, cdim), lambda b: (b, 0, 0)),
            pl.BlockSpec((None, t, d_inner), lambda b: (b, 0, 0)),
            pl.BlockSpec((None, nheads, SSM_HEAD_DIM, D_STATE), lambda b: (b, 0, 0, 0)),
            small(cw.shape), small(cb.shape), small(dtbx.shape), small(alogx.shape), small(dskx.shape),
        ],
        out_specs=[
            pl.BlockSpec((None, t, d_inner), lambda b: (b, 0, 0)),
            pl.BlockSpec((None, nheads, SSM_HEAD_DIM, D_STATE), lambda b: (b, 0, 0, 0)),
        ],
        out_shape=[
            jax.ShapeDtypeStruct((bs, t, d_inner), F32),
            jax.ShapeDtypeStruct((bs, nheads, SSM_HEAD_DIM, D_STATE), F32),
        ],
        scratch_shapes=[
            pltpu.VMEM((LANES, d_inner), F32),
            pltpu.VMEM((SSM_GROUPS, LANES, D_STATE), F32),
            pltpu.VMEM((SSM_GROUPS, LANES, D_STATE), F32),
        ],
        compiler_params=_params(("parallel",)),
        name="ssd_sample",
    )(xf, dtx, h0, cw, cb, dtbx, alogx, dskx)


def _merge_kernel(o_ref, za_ref, y_ref, zm_ref, g_ref, x_ref, p_ref, wa_ref, ws_ref, wo_ref, wpg_ref, wp_ref,
                  snw_ref, pnw_ref, fnw_ref, out_ref, *, final):
    d = x_ref.shape[1]
    ya = _dot((o_ref[...] * _silu(za_ref[...])).astype(BF16), wa_ref[...])
    yz = y_ref[...] * _silu(zm_ref[...])
    gw = yz.shape[1] // SSM_GROUPS
    parts = []
    for g in range(SSM_GROUPS):
        seg = yz[:, g * gw:(g + 1) * gw]
        parts.append(seg * lax.rsqrt(jnp.mean(seg * seg, axis=-1, keepdims=True) + EPS))
    yn = jnp.concatenate(parts, axis=1) * snw_ref[...]
    ys = _dot(yn.astype(BF16), ws_ref[...])
    mix = _sigmoid(g_ref[:, :d]) * ya + _sigmoid(g_ref[:, d:]) * ys
    h1 = x_ref[...] + _dot(mix.astype(BF16), wo_ref[...])
    pg = _sigmoid(_dot(_rms(h1, pnw_ref[...]).astype(BF16), wpg_ref[...]))
    h2 = h1 + _dot(p_ref[...].astype(BF16), wp_ref[...]) * pg
    out_ref[...] = _rms(h2, fnw_ref[...]) if final else h2


def _merge(o, za, y, zm, gl, x, p, weights, norms, tm, final):
    n, d = x.shape
    assert n % tm == 0
    row = lambda arr: pl.BlockSpec((tm, arr.shape[1]), lambda i: (i, 0))
    whole = lambda arr: pl.BlockSpec(arr.shape, lambda i: (0, 0))
    acts = [o, za, y, zm, gl, x, p]
    return pl.pallas_call(
        functools.partial(_merge_kernel, final=final),
        grid=(n // tm,),
        in_specs=[row(a) for a in acts] + [whole(w) for w in weights] + [whole(w) for w in norms],
        out_specs=pl.BlockSpec((tm, d), lambda i: (i, 0)),
        out_shape=jax.ShapeDtypeStruct((n, d), F32),
        compiler_params=_params(("parallel",)),
        name="merge_out",
    )(*acts, *weights, *norms)


def _layer(h, p_l, q_start, past, conv_state, ssm_state, lw, final_norm_w, final):
    (norm_mix_w, w_in, conv_w, conv_b, dt_bias, a_log, d_skip, ssm_norm_w,
     w_attn_br, w_ssm_br, w_out, ple_norm_w, w_ple_gate, w_ple) = lw
    bsz, t, d = h.shape
    a = w_attn_br.shape[0]
    d_inner = w_ssm_br.shape[0]
    nheads = dt_bias.shape[0]
    cdim = conv_w.shape[1]
    n = bsz * t
    row = lambda v: v.reshape(1, -1).astype(F32)

    sizes = (a, a, a, a, d_inner, cdim, nheads, 2 * d)
    offs = [0]
    for s in sizes:
        offs.append(offs[-1] + s)
    cols = [w_in[:, offs[i]:offs[i + 1]] for i in range(len(sizes))]
    cols[6] = jnp.pad(cols[6], ((0, 0), (0, LANES - nheads)))
    widths = (a, a, a, a, d_inner, cdim, LANES, 2 * d)
    w_pad = jnp.concatenate(cols, axis=1).astype(BF16)

    cos, sin = _rope_tables(t, q_start)
    prompt = past is None
    if prompt:
        tm = MOBA_BLOCK
        tiles_per_seq = t // tm
        cos_map = lambda i: (i % tiles_per_seq, 0)
    else:
        tm = n if n <= 256 else 256
        assert tm % t == 0
        cos = jnp.tile(cos, (tm // t, 1))
        sin = jnp.tile(sin, (tm // t, 1))
        cos_map = lambda i: (0, 0)
    outs = _inproj(h.reshape(n, d), row(norm_mix_w), w_pad, cos, sin, widths, tm, cos_map, prompt)
    q, k, v, za, zm, xbc, dtp, gl = outs[:8]
    q3, k3, v3 = (z.reshape(bsz, t, a) for z in (q, k, v))

    cw = conv_w.astype(F32)
    cb = row(conv_b)
    if prompt:
        kmean = outs[8].reshape(bsz, t // MOBA_BLOCK, a)
        o = _moba_prompt(q3, k3, v3, kmean)
        pad = lambda vec: jnp.pad(row(vec), ((0, 0), (0, LANES - nheads)))
        y_ssd, ssm_new = _ssd_prompt(xbc.reshape(bsz, t, cdim), dtp.reshape(bsz, t, LANES), cw, cb,
                                     pad(dt_bias), pad(a_log), pad(d_skip), d_inner, nheads)
        xbc_full_tail = xbc.reshape(bsz, t, cdim)[:, t - (CONV_W - 1):, :]
        if t < CONV_W - 1:
            raise NotImplementedError("prompt shorter than the conv window")
        conv_new = xbc_full_tail
    else:
        cache_k, cache_v, page_table = past
        o = _moba_sample(q3, k3, v3, cache_k, cache_v, page_table)
        xf = jnp.concatenate([conv_state.astype(F32), xbc.reshape(bsz, t, cdim)], axis=1)
        conv_new = xf[:, -(CONV_W - 1):, :]
        rep = lambda vec: jnp.repeat(row(vec), SSM_HEAD_DIM, axis=1)
        dtx = jnp.repeat(dtp.reshape(bsz, t, LANES)[:, :, :nheads], SSM_HEAD_DIM, axis=2)
        y_ssd, ssm_new = _ssd_sample(xf, dtx, ssm_state.astype(F32), cw, cb, rep(dt_bias), rep(a_log),
                                     rep(d_skip), t, d_inner, nheads)

    weights = [w.astype(BF16) for w in (w_attn_br, w_ssm_br, w_out, w_ple_gate, w_ple)]
    norms = [row(ssm_norm_w), row(ple_norm_w), row(final_norm_w)]
    tm_out = 256 if n % 256 == 0 else n
    h_out = _merge(o.reshape(n, a), za, y_ssd.reshape(n, d_inner), zm, gl, h.reshape(n, d),
                   p_l.reshape(n, -1).astype(F32), weights, norms, tm_out, final)
    nh_attn = a // HEAD_DIM
    return (h_out.reshape(bsz, t, d), k3.reshape(bsz, t, nh_attn, HEAD_DIM), v3.reshape(bsz, t, nh_attn, HEAD_DIM),
            conv_new, ssm_new)


def kernel(x_prompt, x_sample, cache_k, cache_v, state_conv, state_ssm, page_table, p_prompt, p_sample,
           norm_mix_w, w_in, conv_w, conv_b, dt_bias, a_log, d_skip, ssm_norm_w,
           w_attn_br, w_ssm_br, w_out, ple_norm_w, w_ple_gate, w_ple, final_norm_w):
    depth = w_in.shape[0]
    past_len = page_table.shape[1] * cache_k.shape[2]
    h_p, h_s = x_prompt, x_sample
    acc = [[] for _ in range(8)]
    for l in range(depth):
        lw = (norm_mix_w[l], w_in[l], conv_w[l], conv_b[l], dt_bias[l], a_log[l], d_skip[l], ssm_norm_w[l],
              w_attn_br[l], w_ssm_br[l], w_out[l], ple_norm_w[l], w_ple_gate[l], w_ple[l])
        final = l == depth - 1
        h_p, kp, vp, cp, sp = _layer(h_p, p_prompt[l], 0, None, None, None, lw, final_norm_w, final)
        h_s, ks, vs, cs, ss = _layer(h_s, p_sample[l], past_len, (cache_k[l], cache_v[l], page_table),
                                     state_conv[l], state_ssm[l], lw, final_norm_w, final)
        for lst, val in zip(acc, (kp, vp, cp, sp, ks, vs, cs, ss)):
            lst.append(val)
    return (h_p, h_s) + tuple(jnp.stack(lst) for lst in acc)
```

```python
import functools

import jax
import jax.numpy as jnp
from jax import lax
from jax.experimental import pallas as pl
from jax.experimental.pallas import tpu as pltpu

F32 = jnp.float32
BF16 = jnp.bfloat16

HEAD_DIM = 64
MOBA_BLOCK = 256
MOBA_TOPK = 3
ROPE_THETA = 10000.0
SSM_HEAD_DIM = 64
SSM_GROUPS = 2
D_STATE = 128
CONV_W = 4
SSD_CHUNK = 128
EPS = 1e-6
NEG_INF = -1e30

LANES = 128
SUBLANES = 8
VMEM_LIMIT_BYTES = 56 * 1024 * 1024


def _sigmoid(x):
    return 1.0 / (1.0 + jnp.exp(-x))


def _silu(x):
    return x * _sigmoid(x)


def _softplus(x):
    return jnp.maximum(x, 0.0) + jnp.log1p(jnp.exp(-jnp.abs(x)))


def _rms(x, w):
    ms = jnp.mean(x * x, axis=-1, keepdims=True)
    return x * lax.rsqrt(ms + EPS) * w


def _dot(a, b):
    return jnp.dot(a, b, preferred_element_type=F32)


def _dot_nt(a, b):
    return lax.dot_general(a, b, (((1,), (1,)), ((), ())), preferred_element_type=F32)


def _split2(x):
    hi = x.astype(BF16)
    lo = (x - hi.astype(F32)).astype(BF16)
    return hi, lo


def _split3(x):
    b1 = x.astype(BF16)
    r1 = x - b1.astype(F32)
    b2 = r1.astype(BF16)
    b3 = (r1 - b2.astype(F32)).astype(BF16)
    return b1, b2, b3


def _params(sem):
    return pltpu.CompilerParams(dimension_semantics=sem, vmem_limit_bytes=VMEM_LIMIT_BYTES)


def _rope_table_kernel(inv_ref, cos_ref, sin_ref, *, start, rows):
    i = pl.program_id(0)
    pos = lax.broadcasted_iota(jnp.int32, (rows, LANES), 0) + (start + i * rows)
    lane = lax.broadcasted_iota(jnp.int32, (rows, LANES), 1)
    ang = pos.astype(F32) * inv_ref[...]
    first_half = (lane % HEAD_DIM) < (HEAD_DIM // 2)
    cos_ref[...] = jnp.cos(ang)
    s = jnp.sin(ang)
    sin_ref[...] = jnp.where(first_half, -s, s)


def _rope_tables(t, start):
    half = HEAD_DIM // 2
    inv = ROPE_THETA ** (-jnp.arange(half, dtype=F32) / half)
    inv = jnp.tile(inv, LANES // half)[None, :]
    rows = min(t, 512)
    assert t % rows == 0
    spec = pl.BlockSpec((rows, LANES), lambda i: (i, 0))
    return pl.pallas_call(
        functools.partial(_rope_table_kernel, start=start, rows=rows),
        grid=(t // rows,),
        in_specs=[pl.BlockSpec((1, LANES), lambda i: (0, 0))],
        out_specs=[spec, spec],
        out_shape=[jax.ShapeDtypeStruct((t, LANES), F32)] * 2,
        compiler_params=_params(("parallel",)),
        name="rope_tables",
    )(inv)


def _rope(x, cos, sin_signed):
    lane = lax.broadcasted_iota(jnp.int32, cos.shape, 1)
    first_half = (lane % HEAD_DIM) < (HEAD_DIM // 2)
    half = HEAD_DIM // 2
    outs = []
    for c in range(x.shape[1] // LANES):
        xs = x[:, c * LANES:(c + 1) * LANES]
        partner = jnp.where(first_half, pltpu.roll(xs, LANES - half, 1), pltpu.roll(xs, half, 1))
        outs.append(xs * cos + partner * sin_signed)
    return jnp.concatenate(outs, axis=1)


def _inproj_kernel(x_ref, nw_ref, w_ref, cos_ref, sin_ref, *out_refs, widths, with_kmean):
    u = _rms(x_ref[...], nw_ref[...]).astype(BF16)
    cos = cos_ref[...]
    sin = sin_ref[...]
    off = 0
    for idx, wd in enumerate(widths):
        r = _dot(u, w_ref[:, off:off + wd])
        off += wd
        if idx < 2:
            r = _rope(r, cos, sin)
        out_refs[idx][...] = r
        if idx == 1 and with_kmean:
            out_refs[len(widths)][...] = jnp.mean(r, axis=0, keepdims=True)


def _inproj(x2d, nw, w_pad, cos, sin, widths, tm, cos_map, with_kmean):
    n, d = x2d.shape
    assert n % tm == 0
    out_shape = [jax.ShapeDtypeStruct((n, wd), F32) for wd in widths]
    out_specs = [pl.BlockSpec((tm, wd), lambda i: (i, 0)) for wd in widths]
    if with_kmean:
        out_shape.append(jax.ShapeDtypeStruct((n // tm, 1, widths[1]), F32))
        out_specs.append(pl.BlockSpec((None, 1, widths[1]), lambda i: (i, 0, 0)))
    return pl.pallas_call(
        functools.partial(_inproj_kernel, widths=widths, with_kmean=with_kmean),
        grid=(n // tm,),
        in_specs=[
            pl.BlockSpec((tm, d), lambda i: (i, 0)),
            pl.BlockSpec((1, d), lambda i: (0, 0)),
            pl.BlockSpec(w_pad.shape, lambda i: (0, 0)),
            pl.BlockSpec((tm, LANES), cos_map),
            pl.BlockSpec((tm, LANES), cos_map),
        ],
        out_specs=out_specs,
        out_shape=out_shape,
        compiler_params=_params(("parallel",)),
        name="inproj",
    )(x2d, nw, w_pad, cos, sin)


KV_GROUP = 2
LOG2E = 1.4426950408889634


def _top3_rows(gate, valid, blk):
    g = jnp.where(valid, gate, NEG_INF)
    sel = jnp.zeros(g.shape, F32)
    big = jnp.float32(1e9)
    for _ in range(MOBA_TOPK):
        mx = jnp.max(g, axis=0, keepdims=True)
        idx = jnp.min(jnp.where(g == mx, blk, big), axis=0, keepdims=True)
        pick = blk == idx
        sel = jnp.where(pick, 1.0, sel)
        g = jnp.where(pick, -jnp.inf, g)
    return jnp.where(valid, sel, 0.0)


def _moba_prompt_kernel(q_ref, kown_ref, vown_ref, k_ref, vt_ref, km_ref, o_ref, *, nblk):
    tq = MOBA_BLOCK
    half = LANES // 2
    own = pl.program_id(2)
    q = q_ref[...]
    km_hi, km_lo = _split2(km_ref[...])
    lane = lax.broadcasted_iota(jnp.int32, (1, LANES), 1)
    blk = lax.broadcasted_iota(jnp.int32, (nblk, tq), 0).astype(F32)
    zpad = jnp.zeros((half, tq), F32)
    zq = jnp.zeros((tq, LANES), BF16)
    q_plain = []
    q_aug = []
    for hh in range(2):
        mine = (lane // HEAD_DIM) == hh
        qh = jnp.where(mine, q, 0.0)
        q_hi, q_lo = _split2(qh)
        gate = _dot_nt(km_hi, q_hi) + _dot_nt(km_hi, q_lo) + _dot_nt(km_lo, q_hi)
        sel = _top3_rows(gate, blk < own.astype(F32), blk)
        bias = jnp.where(sel > 0.0, 0.0, NEG_INF)
        if nblk < half:
            bias = jnp.concatenate([bias, jnp.zeros((half - nblk, tq), F32)], axis=0)
        bias_t = (jnp.concatenate([zpad, bias], axis=0) if hh == 0 else jnp.concatenate([bias, zpad], axis=0)).T
        qpl = qh * (HEAD_DIM ** -0.5 * LOG2E)
        q_plain.append(qpl.astype(BF16))
        q_aug.append(jnp.where(mine, qpl, bias_t).astype(BF16))
    blockdiag = lambda a, b: jnp.concatenate(
        [jnp.concatenate([a, zq], axis=1), jnp.concatenate([zq, b], axis=1)], axis=0)
    q2_aug = blockdiag(q_aug[0], q_aug[1])

    def pv(vts, p):
        return jnp.concatenate([_dot(vts[0], p[:, :tq]), _dot(vts[1], p[:, tq:])], axis=1)

    kpos = lax.broadcasted_iota(jnp.int32, (tq, 2 * tq), 0)
    qpos = lax.broadcasted_iota(jnp.int32, (tq, 2 * tq), 1) % tq
    s = jnp.where(kpos <= qpos, _dot_nt(kown_ref[...], blockdiag(q_plain[0], q_plain[1])), NEG_INF)
    m0 = jnp.max(s, axis=0, keepdims=True)
    acc0 = pv((vown_ref[0], vown_ref[1]), jnp.exp2(s - m0).astype(BF16))

    rows_g = KV_GROUP * MOBA_BLOCK
    last_g = nblk // KV_GROUP - 1

    def body(it, carry):
        m, acc = carry
        ga = 2 * it
        gb = jnp.minimum(ga + 1, last_g)
        sa = _dot_nt(k_ref[pl.ds(pl.multiple_of(ga * rows_g, rows_g), rows_g), :], q2_aug)
        sb = _dot_nt(k_ref[pl.ds(pl.multiple_of(gb * rows_g, rows_g), rows_g), :], q2_aug)
        sb = jnp.where(ga + 1 <= last_g, sb, NEG_INF)
        for s, g in ((sa, ga), (sb, gb)):
            m_new = jnp.maximum(m, jnp.max(s, axis=0, keepdims=True))
            p = jnp.exp2(s - m_new).astype(BF16)
            acc = jnp.exp2(m - m_new) * acc + pv((vt_ref[0, g], vt_ref[1, g]), p)
            m = m_new
        return m, acc

    ngroups = (own + KV_GROUP - 1) // KV_GROUP
    _, acc = lax.fori_loop(0, (ngroups + 1) // 2, body, (m0, acc0))
    for hh in range(2):
        a_h = acc[:, hh * tq:(hh + 1) * tq]
        o_ref[hh * HEAD_DIM:(hh + 1) * HEAD_DIM, :] = a_h[0:HEAD_DIM, :] / a_h[HEAD_DIM:HEAD_DIM + 1, :]


def _moba_prompt(q, k, v, kmean):
    bsz, t, a = q.shape
    nblk = t // MOBA_BLOCK
    nheads = a // HEAD_DIM
    half = LANES // 2
    assert t % MOBA_BLOCK == 0 and a % LANES == 0 and 2 * HEAD_DIM == LANES
    assert nblk % SUBLANES == 0 and nblk <= half and nblk % KV_GROUP == 0
    ng = nblk // KV_GROUP
    kh = k.astype(BF16).reshape(bsz, t, nheads // 2, 2, HEAD_DIM).transpose(0, 2, 1, 3, 4)
    onehot = (jnp.arange(t)[:, None] // MOBA_BLOCK == jnp.arange(half)[None, :]).astype(BF16)
    onehot = jnp.broadcast_to(onehot, (bsz, nheads // 2, t, half))
    kaug = jnp.concatenate([kh[:, :, :, 0, :], onehot, onehot, kh[:, :, :, 1, :]], axis=-1)
    vh_t = v.astype(BF16).reshape(bsz, t, nheads, HEAD_DIM).transpose(0, 2, 3, 1)
    vaug_t = jnp.concatenate([vh_t, jnp.ones((bsz, nheads, 1, t), BF16),
                              jnp.zeros((bsz, nheads, LANES - HEAD_DIM - 1, t), BF16)], axis=2)
    vaug_g = vaug_t.reshape(bsz, nheads, LANES, ng, KV_GROUP * MOBA_BLOCK).transpose(0, 1, 3, 2, 4)
    o_t = pl.pallas_call(
        functools.partial(_moba_prompt_kernel, nblk=nblk),
        grid=(bsz, nheads // 2, nblk),
        in_specs=[
            pl.BlockSpec((None, MOBA_BLOCK, LANES), lambda b, hp, i: (b, i, hp)),
            pl.BlockSpec((None, None, MOBA_BLOCK, 2 * LANES), lambda b, hp, i: (b, hp, i, 0)),
            pl.BlockSpec((None, 2, LANES, MOBA_BLOCK), lambda b, hp, i: (b, hp, 0, i)),
            pl.BlockSpec((None, None, t, 2 * LANES), lambda b, hp, i: (b, hp, 0, 0)),
            pl.BlockSpec((None, 2, ng, LANES, KV_GROUP * MOBA_BLOCK), lambda b, hp, i: (b, hp, 0, 0, 0)),
            pl.BlockSpec((None, nblk, LANES), lambda b, hp, i: (b, 0, hp)),
        ],
        out_specs=pl.BlockSpec((None, LANES, MOBA_BLOCK), lambda b, hp, i: (b, hp, i)),
        out_shape=jax.ShapeDtypeStruct((bsz, a, t), F32),
        compiler_params=_params(("parallel", "parallel", "arbitrary")),
        name="moba_prompt",
    )(q, kaug, vaug_t, kaug, vaug_g, kmean)
    return o_t.transpose(0, 2, 1)


def _moba_sample_kernel(pt_ref, q_ref, kn_ref, vn_ref, *refs, pps, tq, nheads):
    del pt_ref
    k_pages = refs[:pps]
    v_pages = refs[pps:2 * pps]
    o_ref = refs[2 * pps]
    g_sc, m_sc, l_sc, o_sc = refs[2 * pps + 1:]
    c = pl.program_id(1)
    rows = nheads * tq
    page = k_pages[0].shape[0]
    per_blk = MOBA_BLOCK // page
    ncol = MOBA_BLOCK * nheads
    q = q_ref[...]
    q_hi, q_lo = _split2(q)
    qs = (q * (HEAD_DIM ** -0.5)).astype(BF16)
    row_h = lax.broadcasted_iota(jnp.int32, (rows, ncol), 0) // tq
    col_h = lax.broadcasted_iota(jnp.int32, (rows, ncol), 1) % nheads
    same_head = row_h == col_h
    grow_h = lax.broadcasted_iota(jnp.int32, (rows, nheads), 0) // tq
    gcol_h = lax.broadcasted_iota(jnp.int32, (rows, nheads), 1)
    nblk_step = pps // per_blk
    flat = lambda pages: jnp.concatenate([x[...].reshape(page * nheads, HEAD_DIM) for x in pages], axis=0)
    scores = []
    for jj in range(nblk_step):
        kp = k_pages[jj * per_blk:(jj + 1) * per_blk]
        scores.append(_dot_nt(qs, flat(kp).astype(BF16)))
        km = sum(jnp.sum(x[...], axis=0) for x in kp) * (1.0 / MOBA_BLOCK)
        km_hi, km_lo = _split2(km)
        gfull = _dot_nt(q_hi, km_hi) + _dot_nt(q_lo, km_hi) + _dot_nt(q_hi, km_lo)
        g_sc[c * nblk_step + jj] = jnp.sum(jnp.where(grow_h == gcol_h, gfull, 0.0), axis=1, keepdims=True)
    probs = []
    for jj in range(nblk_step):
        s = jnp.where(same_head, scores[jj], NEG_INF)
        m = jnp.max(s, axis=1, keepdims=True)
        p = jnp.exp(s - m)
        m_sc[c * nblk_step + jj] = m
        l_sc[c * nblk_step + jj] = jnp.sum(p, axis=1, keepdims=True)
        probs.append(p.astype(BF16))
    for jj in range(nblk_step):
        vj = flat(v_pages[jj * per_blk:(jj + 1) * per_blk]).astype(BF16)
        o_sc[c * nblk_step + jj] = _dot(probs[jj], vj)

    @pl.when(c == pl.num_programs(1) - 1)
    def _():
        gates = g_sc[...]
        blk3 = lax.broadcasted_iota(jnp.int32, gates.shape, 0).astype(F32)
        sel = _top3_rows(gates, blk3 >= 0.0, blk3) > 0.0
        m_all = m_sc[...]
        t_idx = lax.broadcasted_iota(jnp.int32, (rows, 1), 0) % tq
        s_own = []
        m_own = jnp.full((rows, 1), NEG_INF, F32)
        for kk in range(tq):
            sk = jnp.sum(q * (HEAD_DIM ** -0.5) * kn_ref[kk], axis=1, keepdims=True)
            sk = jnp.where(t_idx >= kk, sk, NEG_INF)
            s_own.append(sk)
            m_own = jnp.maximum(m_own, sk)
        m_fin = jnp.maximum(m_own, jnp.max(jnp.where(sel, m_all, NEG_INF), axis=0))
        w = jnp.exp(jnp.where(sel, m_all - m_fin, NEG_INF))
        l_fin = jnp.sum(w * l_sc[...], axis=0)
        o_fin = jnp.sum(w * o_sc[...], axis=0)
        for kk in range(tq):
            pk = jnp.exp(s_own[kk] - m_fin)
            l_fin = l_fin + pk
            o_fin = o_fin + pk * vn_ref[kk]
        o_ref[...] = o_fin / l_fin


def _moba_sample(q, k_new, v_new, cache_k, cache_v, layer, page_table):
    bs, tq, a = q.shape
    page, nheads = cache_k.shape[2], cache_k.shape[3]
    n_pages = page_table.shape[1]
    past = n_pages * page
    assert cache_k.shape[4] == HEAD_DIM and a == nheads * HEAD_DIM
    assert MOBA_BLOCK % page == 0 and past % MOBA_BLOCK == 0 and tq <= MOBA_BLOCK
    per_blk = MOBA_BLOCK // page
    nb = past // MOBA_BLOCK
    assert nb >= MOBA_TOPK
    pps = per_blk * 4
    while n_pages % pps:
        pps -= per_blk
    rows = nheads * tq
    heads_first = lambda z: z.reshape(bs, tq, nheads, HEAD_DIM).transpose(0, 2, 1, 3)
    q_rows = heads_first(q).reshape(bs, rows, HEAD_DIM)
    rep = lambda z: jnp.repeat(z.reshape(bs, tq, nheads, 1, HEAD_DIM), tq, axis=3).reshape(bs, tq, rows, HEAD_DIM)

    def page_spec(r):
        return pl.BlockSpec((None, None, page, nheads, HEAD_DIM),
                            lambda b, c, pt: (layer, pt[b, c * pps + r], 0, 0, 0))

    new_spec = pl.BlockSpec((None, tq, rows, HEAD_DIM), lambda b, c, pt: (b, 0, 0, 0))
    row_spec = pl.BlockSpec((None, rows, HEAD_DIM), lambda b, c, pt: (b, 0, 0))
    o_rows = pl.pallas_call(
        functools.partial(_moba_sample_kernel, pps=pps, tq=tq, nheads=nheads),
        grid_spec=pltpu.PrefetchScalarGridSpec(
            num_scalar_prefetch=1,
            grid=(bs, n_pages // pps),
            in_specs=[row_spec, new_spec, new_spec] + [page_spec(r) for r in range(pps)] * 2,
            out_specs=row_spec,
            scratch_shapes=[
                pltpu.VMEM((nb, rows, 1), F32),
                pltpu.VMEM((nb, rows, 1), F32),
                pltpu.VMEM((nb, rows, 1), F32),
                pltpu.VMEM((nb, rows, HEAD_DIM), F32),
            ],
        ),
        out_shape=jax.ShapeDtypeStruct((bs, rows, HEAD_DIM), F32),
        compiler_params=_params(("parallel", "arbitrary")),
        name="moba_sample",
    )(page_table, q_rows, rep(k_new), rep(v_new), *([cache_k] * pps), *([cache_v] * pps))
    return o_rows.reshape(bs, nheads, tq, HEAD_DIM).transpose(0, 2, 1, 3).reshape(bs, tq, a)


def _ssd_prompt_kernel(xbc_ref, dt_ref, cw_ref, cb_ref, dtb_ref, alog_ref, dsk_ref, y_ref, hout_ref,
                       xwin_sc, h_sc, *, d_inner, nheads):
    cl = SSD_CHUNK
    c = pl.program_id(1)

    @pl.when(c == 0)
    def _():
        xwin_sc[0:SUBLANES, :] = jnp.zeros((SUBLANES, xwin_sc.shape[1]), F32)
        h_sc[...] = jnp.zeros(h_sc.shape, F32)

    cur = xbc_ref[...]
    xwin_sc[SUBLANES:SUBLANES + cl, :] = cur
    acc = jnp.broadcast_to(cb_ref[...], cur.shape)
    for i in range(CONV_W):
        acc = acc + xwin_sc[pl.ds(SUBLANES - (CONV_W - 1) + i, cl), :] * cw_ref[i:i + 1, :]
    xwin_sc[0:SUBLANES, :] = cur[cl - SUBLANES:cl, :]
    xc = _silu(acc)
    gn = SSM_GROUPS * D_STATE
    xs = xc[:, :d_inner]
    bm = xc[:, d_inner:d_inner + gn]
    cm = xc[:, d_inner + gn:d_inner + 2 * gn]

    dt = _softplus(dt_ref[...] + dtb_ref[...])
    da = dt * (-jnp.exp(alog_ref[...]))
    li = lax.broadcasted_iota(jnp.int32, (cl, cl), 0)
    si = lax.broadcasted_iota(jnp.int32, (cl, cl), 1)
    causal = li >= si
    tril = jnp.where(causal, 1.0, 0.0).astype(BF16)
    d1, d2, d3 = _split3(da)
    acs = _dot(tril, d1) + _dot(tril, d2) + _dot(tril, d3)
    acs_t = acs.T
    dt_t = dt.T
    xs_t = xs.T
    lane = lax.broadcasted_iota(jnp.int32, (1, LANES), 1)
    lo_half = lane < SSM_HEAD_DIM
    hpg = nheads // SSM_GROUPS
    ys = []
    for hp in range(nheads // 2):
        g = (2 * hp) // hpg
        bg = bm[:, g * D_STATE:(g + 1) * D_STATE]
        cg = cm[:, g * D_STATE:(g + 1) * D_STATE].astype(BF16)
        cb = _dot_nt(cg, bg.astype(BF16))
        x2 = xs[:, hp * LANES:(hp + 1) * LANES]
        x2b = x2.astype(BF16)
        yd = []
        ecol = []
        for hh in range(2):
            h = 2 * hp + hh
            acol = acs[:, h:h + 1]
            diff = acol - acs_t[h:h + 1, :]
            lm = jnp.exp(jnp.where(causal, diff, NEG_INF))
            mh = cb * lm * dt_t[h:h + 1, :]
            yd.append(_dot(mh.astype(BF16), x2b))
            ecol.append(jnp.exp(acol))
            alast = acs[cl - 1:cl, h:h + 1]
            wcol = jnp.exp(alast - acol) * dt[:, h:h + 1]
            st = _dot(xs_t[h * SSM_HEAD_DIM:(h + 1) * SSM_HEAD_DIM, :].astype(BF16), (bg * wcol).astype(BF16))
            hprev = h_sc[h]
            if hh == 0:
                hprev0 = hprev
            else:
                hprev2 = jnp.concatenate([hprev0, hprev], axis=0).astype(BF16)
            h_sc[h] = jnp.exp(alast) * hprev + st
        yo = _dot_nt(cg, hprev2) * jnp.where(lo_half, ecol[0], ecol[1])
        dsk2 = jnp.where(lo_half, dsk_ref[:, 2 * hp:2 * hp + 1], dsk_ref[:, 2 * hp + 1:2 * hp + 2])
        ys.append(jnp.where(lo_half, yd[0], yd[1]) + yo + dsk2 * x2)
    y_ref[...] = jnp.concatenate(ys, axis=1)

    @pl.when(c == pl.num_programs(1) - 1)
    def _():
        hout_ref[...] = h_sc[...]


def _ssd_prompt(xbc, dtp, cw, cb, dtb, alog, dsk, d_inner, nheads):
    bsz, t, cdim = xbc.shape
    assert t % SSD_CHUNK == 0 and nheads % 2 == 0 and (nheads // SSM_GROUPS) % 2 == 0
    nc = t // SSD_CHUNK
    small = lambda shape: pl.BlockSpec(shape, lambda b, c: (0, 0))
    return pl.pallas_call(
        functools.partial(_ssd_prompt_kernel, d_inner=d_inner, nheads=nheads),
        grid=(bsz, nc),
        in_specs=[
            pl.BlockSpec((None, SSD_CHUNK, cdim), lambda b, c: (b, c, 0)),
            pl.BlockSpec((None, SSD_CHUNK, LANES), lambda b, c: (b, c, 0)),
            small(cw.shape), small(cb.shape), small(dtb.shape), small(alog.shape), small(dsk.shape),
        ],
        out_specs=[
            pl.BlockSpec((None, SSD_CHUNK, d_inner), lambda b, c: (b, c, 0)),
            pl.BlockSpec((None, nheads, SSM_HEAD_DIM, D_STATE), lambda b, c: (b, 0, 0, 0)),
        ],
        out_shape=[
            jax.ShapeDtypeStruct((bsz, t, d_inner), F32),
            jax.ShapeDtypeStruct((bsz, nheads, SSM_HEAD_DIM, D_STATE), F32),
        ],
        scratch_shapes=[
            pltpu.VMEM((SUBLANES + SSD_CHUNK, cdim), F32),
            pltpu.VMEM((nheads, SSM_HEAD_DIM, D_STATE), F32),
        ],
        compiler_params=_params(("parallel", "arbitrary")),
        name="ssd_prompt",
    )(xbc, dtp, cw, cb, dtb, alog, dsk)


def _ssd_sample_kernel(xf_ref, dtx_ref, h0_ref, cw_ref, cb_ref, dtbx_ref, alogx_ref, dskx_ref, y_ref, hout_ref,
                       rows_sc, cpad_sc, bpad_sc, *, t, d_inner, nheads):
    cdim = xf_ref.shape[1]
    acc = jnp.broadcast_to(cb_ref[...], (t, cdim))
    for i in range(CONV_W):
        acc = acc + xf_ref[pl.ds(i, t), :] * cw_ref[i:i + 1, :]
    xc = _silu(acc)
    gn = SSM_GROUPS * D_STATE
    xs = xc[:, :d_inner]
    bm = xc[:, d_inner:d_inner + gn]
    cm = xc[:, d_inner + gn:d_inner + 2 * gn]
    dt = _softplus(dtx_ref[...] + dtbx_ref[...])
    da = dt * (-jnp.exp(alogx_ref[...]))
    acs = [da[0:1, :]]
    for s in range(1, t):
        acs.append(acs[-1] + da[s:s + 1, :])
    lane = lax.broadcasted_iota(jnp.int32, (1, d_inner), 1)
    grp = lane // (d_inner // SSM_GROUPS)

    rows_sc[...] = jnp.zeros(rows_sc.shape, F32)
    for s in range(t):
        rows_sc[s:s + 1, :] = jnp.exp(acs[t - 1] - acs[s]) * dt[s:s + 1, :] * xs[s:s + 1, :]
    rows_sc[t:t + 1, :] = jnp.exp(acs[t - 1])
    cols = rows_sc[...].T
    cpad_sc[...] = jnp.zeros(cpad_sc.shape, F32)
    bpad_sc[...] = jnp.zeros(bpad_sc.shape, F32)
    for g in range(SSM_GROUPS):
        cpad_sc[g, 0:t, :] = cm[:, g * D_STATE:(g + 1) * D_STATE]
        bpad_sc[g, 0:t, :] = bm[:, g * D_STATE:(g + 1) * D_STATE]

    h0 = h0_ref[...].reshape(d_inner, D_STATE)
    half = d_inner // SSM_GROUPS
    yoff_t = []
    hnew = []
    for g in range(SSM_GROUPS):
        h0g = h0[g * half:(g + 1) * half, :]
        yoff_t.append(_dot_nt(h0g.astype(BF16), cpad_sc[g].astype(BF16)))
        colg = cols[g * half:(g + 1) * half, :]
        w_hi, w_lo = _split2(jnp.where(lax.broadcasted_iota(jnp.int32, colg.shape, 1) < t, colg, 0.0))
        b_hi, b_lo = _split2(bpad_sc[g])
        upd = _dot(w_hi, b_hi) + _dot(w_hi, b_lo) + _dot(w_lo, b_hi)
        hnew.append(h0g * colg[:, t:t + 1] + upd)
    hout_ref[...] = jnp.concatenate(hnew, axis=0).reshape(hout_ref.shape)
    yoff = jnp.concatenate(yoff_t, axis=0).T

    for tt in range(t):
        y = yoff[tt:tt + 1, :] * jnp.exp(acs[tt]) + dskx_ref[...] * xs[tt:tt + 1, :]
        for s in range(tt + 1):
            cbs = []
            for g in range(SSM_GROUPS):
                sl = slice(g * D_STATE, (g + 1) * D_STATE)
                cbs.append(jnp.sum(cm[tt:tt + 1, sl] * bm[s:s + 1, sl], axis=1, keepdims=True))
            cbx = jnp.where(grp == 0, cbs[0], cbs[1])
            y = y + cbx * jnp.exp(acs[tt] - acs[s]) * dt[s:s + 1, :] * xs[s:s + 1, :]
        y_ref[tt:tt + 1, :] = y


def _ssd_sample(xf, dtx, h0, cw, cb, dtbx, alogx, dskx, t, d_inner, nheads):
    bs, tf, cdim = xf.shape
    assert SSM_GROUPS == 2 and t + 1 <= LANES
    small = lambda shape: pl.BlockSpec(shape, lambda b: (0, 0))
    return pl.pallas_call(
        functools.partial(_ssd_sample_kernel, t=t, d_inner=d_inner, nheads=nheads),
        grid=(bs,),
        in_specs=[
            pl.BlockSpec((None, tf, cdim), lambda b: (b, 0, 0)),
            pl.BlockSpec((None, t, d_inner), lambda b: (b, 0, 0)),
            pl.BlockSpec((None, nheads, SSM_HEAD_DIM, D_STATE), lambda b: (b, 0, 0, 0)),
            small(cw.shape), small(cb.shape), small(dtbx.shape), small(alogx.shape), small(dskx.shape),
        ],
        out_specs=[
            pl.BlockSpec((None, t, d_inner), lambda b: (b, 0, 0)),
            pl.BlockSpec((None, nheads, SSM_HEAD_DIM, D_STATE), lambda b: (b, 0, 0, 0)),
        ],
        out_shape=[
            jax.ShapeDtypeStruct((bs, t, d_inner), F32),
            jax.ShapeDtypeStruct((bs, nheads, SSM_HEAD_DIM, D_STATE), F32),
        ],
        scratch_shapes=[
            pltpu.VMEM((LANES, d_inner), F32),
            pltpu.VMEM((SSM_GROUPS, LANES, D_STATE), F32),
            pltpu.VMEM((SSM_GROUPS, LANES, D_STATE), F32),
        ],
        compiler_params=_params(("parallel",)),
        name="ssd_sample",
    )(xf, dtx, h0, cw, cb, dtbx, alogx, dskx)


def _merge_kernel(o_ref, za_ref, y_ref, zm_ref, g_ref, x_ref, p_ref, wa_ref, ws_ref, wo_ref, wpg_ref, wp_ref,
                  snw_ref, pnw_ref, fnw_ref, out_ref, *, final):
    d = x_ref.shape[1]
    ya = _dot((o_ref[...] * _silu(za_ref[...])).astype(BF16), wa_ref[...])
    yz = y_ref[...] * _silu(zm_ref[...])
    gw = yz.shape[1] // SSM_GROUPS
    parts = []
    for g in range(SSM_GROUPS):
        seg = yz[:, g * gw:(g + 1) * gw]
        parts.append(seg * lax.rsqrt(jnp.mean(seg * seg, axis=-1, keepdims=True) + EPS))
    yn = jnp.concatenate(parts, axis=1) * snw_ref[...]
    ys = _dot(yn.astype(BF16), ws_ref[...])
    mix = _sigmoid(g_ref[:, :d]) * ya + _sigmoid(g_ref[:, d:]) * ys
    h1 = x_ref[...] + _dot(mix.astype(BF16), wo_ref[...])
    pg = _sigmoid(_dot(_rms(h1, pnw_ref[...]).astype(BF16), wpg_ref[...]))
    h2 = h1 + _dot(p_ref[...].astype(BF16), wp_ref[...]) * pg
    out_ref[...] = _rms(h2, fnw_ref[...]) if final else h2


def _merge(o, za, y, zm, gl, x, p, weights, norms, tm, final):
    n, d = x.shape
    assert n % tm == 0
    row = lambda arr: pl.BlockSpec((tm, arr.shape[1]), lambda i: (i, 0))
    whole = lambda arr: pl.BlockSpec(arr.shape, lambda i: (0, 0))
    acts = [o, za, y, zm, gl, x, p]
    return pl.pallas_call(
        functools.partial(_merge_kernel, final=final),
        grid=(n // tm,),
        in_specs=[row(a) for a in acts] + [whole(w) for w in weights] + [whole(w) for w in norms],
        out_specs=pl.BlockSpec((tm, d), lambda i: (i, 0)),
        out_shape=jax.ShapeDtypeStruct((n, d), F32),
        compiler_params=_params(("parallel",)),
        name="merge_out",
    )(*acts, *weights, *norms)


def _layer(h, p_l, q_start, past, conv_state, ssm_state, lw, final_norm_w, final):
    (norm_mix_w, w_in, conv_w, conv_b, dt_bias, a_log, d_skip, ssm_norm_w,
     w_attn_br, w_ssm_br, w_out, ple_norm_w, w_ple_gate, w_ple) = lw
    bsz, t, d = h.shape
    a = w_attn_br.shape[0]
    d_inner = w_ssm_br.shape[0]
    nheads = dt_bias.shape[0]
    cdim = conv_w.shape[1]
    n = bsz * t
    row = lambda v: v.reshape(1, -1).astype(F32)

    sizes = (a, a, a, a, d_inner, cdim, nheads, 2 * d)
    offs = [0]
    for s in sizes:
        offs.append(offs[-1] + s)
    cols = [w_in[:, offs[i]:offs[i + 1]] for i in range(len(sizes))]
    cols[6] = jnp.pad(cols[6], ((0, 0), (0, LANES - nheads)))
    widths = (a, a, a, a, d_inner, cdim, LANES, 2 * d)
    w_pad = jnp.concatenate(cols, axis=1).astype(BF16)

    cos, sin = _rope_tables(t, q_start)
    prompt = past is None
    if prompt:
        tm = MOBA_BLOCK
        tiles_per_seq = t // tm
        cos_map = lambda i: (i % tiles_per_seq, 0)
    else:
        tm = n if n <= 256 else 256
        assert tm % t == 0
        cos = jnp.tile(cos, (tm // t, 1))
        sin = jnp.tile(sin, (tm // t, 1))
        cos_map = lambda i: (0, 0)
    outs = _inproj(h.reshape(n, d), row(norm_mix_w), w_pad, cos, sin, widths, tm, cos_map, prompt)
    q, k, v, za, zm, xbc, dtp, gl = outs[:8]
    q3, k3, v3 = (z.reshape(bsz, t, a) for z in (q, k, v))

    cw = conv_w.astype(F32)
    cb = row(conv_b)
    if prompt:
        kmean = outs[8].reshape(bsz, t // MOBA_BLOCK, a)
        o = _moba_prompt(q3, k3, v3, kmean)
        pad = lambda vec: jnp.pad(row(vec), ((0, 0), (0, LANES - nheads)))
        y_ssd, ssm_new = _ssd_prompt(xbc.reshape(bsz, t, cdim), dtp.reshape(bsz, t, LANES), cw, cb,
                                     pad(dt_bias), pad(a_log), pad(d_skip), d_inner, nheads)
        xbc_full_tail = xbc.reshape(bsz, t, cdim)[:, t - (CONV_W - 1):, :]
        if t < CONV_W - 1:
            raise NotImplementedError("prompt shorter than the conv window")
        conv_new = xbc_full_tail
    else:
        cache_k, cache_v, layer, page_table = past
        o = _moba_sample(q3, k3, v3, cache_k, cache_v, layer, page_table)
        xf = jnp.concatenate([conv_state.astype(F32), xbc.reshape(bsz, t, cdim)], axis=1)
        conv_new = xf[:, -(CONV_W - 1):, :]
        rep = lambda vec: jnp.repeat(row(vec), SSM_HEAD_DIM, axis=1)
        dtx = jnp.repeat(dtp.reshape(bsz, t, LANES)[:, :, :nheads], SSM_HEAD_DIM, axis=2)
        y_ssd, ssm_new = _ssd_sample(xf, dtx, ssm_state.astype(F32), cw, cb, rep(dt_bias), rep(a_log),
                                     rep(d_skip), t, d_inner, nheads)

    weights = [w.astype(BF16) for w in (w_attn_br, w_ssm_br, w_out, w_ple_gate, w_ple)]
    norms = [row(ssm_norm_w), row(ple_norm_w), row(final_norm_w)]
    tm_out = 256 if n % 256 == 0 else n
    h_out = _merge(o.reshape(n, a), za, y_ssd.reshape(n, d_inner), zm, gl, h.reshape(n, d),
                   p_l.reshape(n, -1).astype(F32), weights, norms, tm_out, final)
    nh_attn = a // HEAD_DIM
    return (h_out.reshape(bsz, t, d), k3.reshape(bsz, t, nh_attn, HEAD_DIM), v3.reshape(bsz, t, nh_attn, HEAD_DIM),
            conv_new, ssm_new)


def kernel(x_prompt, x_sample, cache_k, cache_v, state_conv, state_ssm, page_table, p_prompt, p_sample,
           norm_mix_w, w_in, conv_w, conv_b, dt_bias, a_log, d_skip, ssm_norm_w,
           w_attn_br, w_ssm_br, w_out, ple_norm_w, w_ple_gate, w_ple, final_norm_w):
    depth = w_in.shape[0]
    past_len = page_table.shape[1] * cache_k.shape[2]
    h_p, h_s = x_prompt, x_sample
    acc = [[] for _ in range(8)]
    for l in range(depth):
        lw = (norm_mix_w[l], w_in[l], conv_w[l], conv_b[l], dt_bias[l], a_log[l], d_skip[l], ssm_norm_w[l],
              w_attn_br[l], w_ssm_br[l], w_out[l], ple_norm_w[l], w_ple_gate[l], w_ple[l])
        final = l == depth - 1
        h_p, kp, vp, cp, sp = _layer(h_p, p_prompt[l], 0, None, None, None, lw, final_norm_w, final)
        h_s, ks, vs, cs, ss = _layer(h_s, p_sample[l], past_len, (cache_k, cache_v, l, page_table),
                                     state_conv[l], state_ssm[l], lw, final_norm_w, final)
        for lst, val in zip(acc, (kp, vp, cp, sp, ks, vs, cs, ss)):
            lst.append(val)
    return (h_p, h_s) + tuple(jnp.stack(lst) for lst in acc)
```

```python
import functools

import jax
import jax.numpy as jnp
from jax import lax
from jax.experimental import pallas as pl
from jax.experimental.pallas import tpu as pltpu

F32 = jnp.float32
BF16 = jnp.bfloat16

HEAD_DIM = 64
MOBA_BLOCK = 256
MOBA_TOPK = 3
ROPE_THETA = 10000.0
SSM_HEAD_DIM = 64
SSM_GROUPS = 2
D_STATE = 128
CONV_W = 4
SSD_CHUNK = 128
EPS = 1e-6
NEG_INF = -1e30

LANES = 128
SUBLANES = 8
VMEM_LIMIT_BYTES = 56 * 1024 * 1024


def _sigmoid(x):
    return 1.0 / (1.0 + jnp.exp(-x))


def _silu(x):
    return x * _sigmoid(x)


def _softplus(x):
    return jnp.maximum(x, 0.0) + jnp.log1p(jnp.exp(-jnp.abs(x)))


def _rms(x, w):
    ms = jnp.mean(x * x, axis=-1, keepdims=True)
    return x * lax.rsqrt(ms + EPS) * w


def _dot(a, b):
    return jnp.dot(a, b, preferred_element_type=F32)


def _dot_nt(a, b):
    return lax.dot_general(a, b, (((1,), (1,)), ((), ())), preferred_element_type=F32)


def _split2(x):
    hi = x.astype(BF16)
    lo = (x - hi.astype(F32)).astype(BF16)
    return hi, lo


def _split3(x):
    b1 = x.astype(BF16)
    r1 = x - b1.astype(F32)
    b2 = r1.astype(BF16)
    b3 = (r1 - b2.astype(F32)).astype(BF16)
    return b1, b2, b3


def _params(sem):
    return pltpu.CompilerParams(dimension_semantics=sem, vmem_limit_bytes=VMEM_LIMIT_BYTES)


def _rope_table_kernel(inv_ref, cos_ref, sin_ref, *, start, rows):
    i = pl.program_id(0)
    pos = lax.broadcasted_iota(jnp.int32, (rows, LANES), 0) + (start + i * rows)
    lane = lax.broadcasted_iota(jnp.int32, (rows, LANES), 1)
    ang = pos.astype(F32) * inv_ref[...]
    first_half = (lane % HEAD_DIM) < (HEAD_DIM // 2)
    cos_ref[...] = jnp.cos(ang)
    s = jnp.sin(ang)
    sin_ref[...] = jnp.where(first_half, -s, s)


def _rope_tables(t, start):
    half = HEAD_DIM // 2
    inv = ROPE_THETA ** (-jnp.arange(half, dtype=F32) / half)
    inv = jnp.tile(inv, LANES // half)[None, :]
    rows = min(t, 512)
    assert t % rows == 0
    spec = pl.BlockSpec((rows, LANES), lambda i: (i, 0))
    return pl.pallas_call(
        functools.partial(_rope_table_kernel, start=start, rows=rows),
        grid=(t // rows,),
        in_specs=[pl.BlockSpec((1, LANES), lambda i: (0, 0))],
        out_specs=[spec, spec],
        out_shape=[jax.ShapeDtypeStruct((t, LANES), F32)] * 2,
        compiler_params=_params(("parallel",)),
        name="rope_tables",
    )(inv)


def _rope(x, cos, sin_signed):
    lane = lax.broadcasted_iota(jnp.int32, cos.shape, 1)
    first_half = (lane % HEAD_DIM) < (HEAD_DIM // 2)
    half = HEAD_DIM // 2
    outs = []
    for c in range(x.shape[1] // LANES):
        xs = x[:, c * LANES:(c + 1) * LANES]
        partner = jnp.where(first_half, pltpu.roll(xs, LANES - half, 1), pltpu.roll(xs, half, 1))
        outs.append(xs * cos + partner * sin_signed)
    return jnp.concatenate(outs, axis=1)


def _inproj_kernel(x_ref, nw_ref, w_ref, cos_ref, sin_ref, *out_refs, widths, with_kmean):
    u = _rms(x_ref[...], nw_ref[...]).astype(BF16)
    cos = cos_ref[...]
    sin = sin_ref[...]
    off = 0
    for idx, wd in enumerate(widths):
        r = _dot(u, w_ref[:, off:off + wd])
        off += wd
        if idx < 2:
            r = _rope(r, cos, sin)
        out_refs[idx][...] = r
        if idx == 1 and with_kmean:
            out_refs[len(widths)][...] = jnp.mean(r, axis=0, keepdims=True)


def _inproj(x2d, nw, w_pad, cos, sin, widths, tm, cos_map, with_kmean):
    n, d = x2d.shape
    assert n % tm == 0
    out_shape = [jax.ShapeDtypeStruct((n, wd), F32) for wd in widths]
    out_specs = [pl.BlockSpec((tm, wd), lambda i: (i, 0)) for wd in widths]
    if with_kmean:
        out_shape.append(jax.ShapeDtypeStruct((n // tm, 1, widths[1]), F32))
        out_specs.append(pl.BlockSpec((None, 1, widths[1]), lambda i: (i, 0, 0)))
    return pl.pallas_call(
        functools.partial(_inproj_kernel, widths=widths, with_kmean=with_kmean),
        grid=(n // tm,),
        in_specs=[
            pl.BlockSpec((tm, d), lambda i: (i, 0)),
            pl.BlockSpec((1, d), lambda i: (0, 0)),
            pl.BlockSpec(w_pad.shape, lambda i: (0, 0)),
            pl.BlockSpec((tm, LANES), cos_map),
            pl.BlockSpec((tm, LANES), cos_map),
        ],
        out_specs=out_specs,
        out_shape=out_shape,
        compiler_params=_params(("parallel",)),
        name="inproj",
    )(x2d, nw, w_pad, cos, sin)


KV_GROUP = 2
LOG2E = 1.4426950408889634


def _top3_rows(gate, valid, blk):
    g = jnp.where(valid, gate, NEG_INF)
    sel = jnp.zeros(g.shape, F32)
    big = jnp.float32(1e9)
    for _ in range(MOBA_TOPK):
        mx = jnp.max(g, axis=0, keepdims=True)
        idx = jnp.min(jnp.where(g == mx, blk, big), axis=0, keepdims=True)
        pick = blk == idx
        sel = jnp.where(pick, 1.0, sel)
        g = jnp.where(pick, -jnp.inf, g)
    return jnp.where(valid, sel, 0.0)


def _moba_prompt_kernel(q_ref, kown_ref, vown_ref, k_ref, vt_ref, km_ref, o_ref, *, nblk):
    tq = MOBA_BLOCK
    half = LANES // 2
    own = pl.program_id(2)
    q = q_ref[...]
    km_hi, km_lo = _split2(km_ref[...])
    lane = lax.broadcasted_iota(jnp.int32, (1, LANES), 1)
    blk = lax.broadcasted_iota(jnp.int32, (nblk, tq), 0).astype(F32)
    zpad = jnp.zeros((half, tq), F32)
    zq = jnp.zeros((tq, LANES), BF16)
    q_plain = []
    q_aug = []
    for hh in range(2):
        mine = (lane // HEAD_DIM) == hh
        qh = jnp.where(mine, q, 0.0)
        q_hi, q_lo = _split2(qh)
        gate = _dot_nt(km_hi, q_hi) + _dot_nt(km_hi, q_lo) + _dot_nt(km_lo, q_hi)
        sel = _top3_rows(gate, blk < own.astype(F32), blk)
        bias = jnp.where(sel > 0.0, 0.0, NEG_INF)
        if nblk < half:
            bias = jnp.concatenate([bias, jnp.zeros((half - nblk, tq), F32)], axis=0)
        bias_t = (jnp.concatenate([zpad, bias], axis=0) if hh == 0 else jnp.concatenate([bias, zpad], axis=0)).T
        qpl = qh * (HEAD_DIM ** -0.5 * LOG2E)
        q_plain.append(qpl.astype(BF16))
        q_aug.append(jnp.where(mine, qpl, bias_t).astype(BF16))
    blockdiag = lambda a, b: jnp.concatenate(
        [jnp.concatenate([a, zq], axis=1), jnp.concatenate([zq, b], axis=1)], axis=0)
    q2_aug = blockdiag(q_aug[0], q_aug[1])

    def pv(vts, p):
        return jnp.concatenate([_dot(vts[0], p[:, :tq]), _dot(vts[1], p[:, tq:])], axis=1)

    kpos = lax.broadcasted_iota(jnp.int32, (tq, 2 * tq), 0)
    qpos = lax.broadcasted_iota(jnp.int32, (tq, 2 * tq), 1) % tq
    s = jnp.where(kpos <= qpos, _dot_nt(kown_ref[...], blockdiag(q_plain[0], q_plain[1])), NEG_INF)
    m0 = jnp.max(s, axis=0, keepdims=True)
    acc0 = pv((vown_ref[0], vown_ref[1]), jnp.exp2(s - m0).astype(BF16))

    rows_g = KV_GROUP * MOBA_BLOCK
    last_g = nblk // KV_GROUP - 1

    def body(it, carry):
        m, acc = carry
        ga = 2 * it
        gb = jnp.minimum(ga + 1, last_g)
        sa = _dot_nt(k_ref[pl.ds(pl.multiple_of(ga * rows_g, rows_g), rows_g), :], q2_aug)
        sb = _dot_nt(k_ref[pl.ds(pl.multiple_of(gb * rows_g, rows_g), rows_g), :], q2_aug)
        sb = jnp.where(ga + 1 <= last_g, sb, NEG_INF)
        for s, g in ((sa, ga), (sb, gb)):
            m_new = jnp.maximum(m, jnp.max(s, axis=0, keepdims=True))
            p = jnp.exp2(s - m_new).astype(BF16)
            acc = jnp.exp2(m - m_new) * acc + pv((vt_ref[0, g], vt_ref[1, g]), p)
            m = m_new
        return m, acc

    ngroups = (own + KV_GROUP - 1) // KV_GROUP
    _, acc = lax.fori_loop(0, (ngroups + 1) // 2, body, (m0, acc0))
    for hh in range(2):
        a_h = acc[:, hh * tq:(hh + 1) * tq]
        o_ref[hh * HEAD_DIM:(hh + 1) * HEAD_DIM, :] = a_h[0:HEAD_DIM, :] / a_h[HEAD_DIM:HEAD_DIM + 1, :]


def _moba_prompt(q, k, v, kmean):
    bsz, t, a = q.shape
    nblk = t // MOBA_BLOCK
    nheads = a // HEAD_DIM
    half = LANES // 2
    assert t % MOBA_BLOCK == 0 and a % LANES == 0 and 2 * HEAD_DIM == LANES
    assert nblk % SUBLANES == 0 and nblk <= half and nblk % KV_GROUP == 0
    ng = nblk // KV_GROUP
    kh = k.astype(BF16).reshape(bsz, t, nheads // 2, 2, HEAD_DIM).transpose(0, 2, 1, 3, 4)
    onehot = (jnp.arange(t)[:, None] // MOBA_BLOCK == jnp.arange(half)[None, :]).astype(BF16)
    onehot = jnp.broadcast_to(onehot, (bsz, nheads // 2, t, half))
    kaug = jnp.concatenate([kh[:, :, :, 0, :], onehot, onehot, kh[:, :, :, 1, :]], axis=-1)
    vh_t = v.astype(BF16).reshape(bsz, t, nheads, HEAD_DIM).transpose(0, 2, 3, 1)
    vaug_t = jnp.concatenate([vh_t, jnp.ones((bsz, nheads, 1, t), BF16),
                              jnp.zeros((bsz, nheads, LANES - HEAD_DIM - 1, t), BF16)], axis=2)
    vaug_g = vaug_t.reshape(bsz, nheads, LANES, ng, KV_GROUP * MOBA_BLOCK).transpose(0, 1, 3, 2, 4)
    o_t = pl.pallas_call(
        functools.partial(_moba_prompt_kernel, nblk=nblk),
        grid=(bsz, nheads // 2, nblk),
        in_specs=[
            pl.BlockSpec((None, MOBA_BLOCK, LANES), lambda b, hp, i: (b, i, hp)),
            pl.BlockSpec((None, None, MOBA_BLOCK, 2 * LANES), lambda b, hp, i: (b, hp, i, 0)),
            pl.BlockSpec((None, 2, LANES, MOBA_BLOCK), lambda b, hp, i: (b, hp, 0, i)),
            pl.BlockSpec((None, None, t, 2 * LANES), lambda b, hp, i: (b, hp, 0, 0)),
            pl.BlockSpec((None, 2, ng, LANES, KV_GROUP * MOBA_BLOCK), lambda b, hp, i: (b, hp, 0, 0, 0)),
            pl.BlockSpec((None, nblk, LANES), lambda b, hp, i: (b, 0, hp)),
        ],
        out_specs=pl.BlockSpec((None, LANES, MOBA_BLOCK), lambda b, hp, i: (b, hp, i)),
        out_shape=jax.ShapeDtypeStruct((bsz, a, t), F32),
        compiler_params=_params(("parallel", "parallel", "arbitrary")),
        name="moba_prompt",
    )(q, kaug, vaug_t, kaug, vaug_g, kmean)
    return o_t.transpose(0, 2, 1)


def _moba_sample_kernel(pt_ref, qrep_ref, kn_ref, vn_ref, *refs, pps, tq, nheads):
    del pt_ref
    k_pages = refs[:pps]
    v_pages = refs[pps:2 * pps]
    o_ref = refs[2 * pps]
    g_sc, m_sc, l_sc, o_sc = refs[2 * pps + 1:]
    c = pl.program_id(1)
    rows = nheads * tq
    a = nheads * HEAD_DIM
    page = k_pages[0].shape[-1]
    per_blk = MOBA_BLOCK // page
    nblk_step = pps // per_blk
    row_h = lax.broadcasted_iota(jnp.int32, (rows, a), 0) // tq
    lane_h = lax.broadcasted_iota(jnp.int32, (rows, a), 1) // HEAD_DIM
    qx = jnp.where(row_h == lane_h, qrep_ref[...], 0.0)
    q_hi, q_lo = _split2(qx)
    qs = (qx * (HEAD_DIM ** -0.5)).astype(BF16)
    flat = lambda x: x[...].reshape(a, page)
    scores = []
    for jj in range(nblk_step):
        kts = [flat(x) for x in k_pages[jj * per_blk:(jj + 1) * per_blk]]
        scores.append(_dot(qs, jnp.concatenate(kts, axis=1).astype(BF16)))
        ks_hi, ks_lo = _split2(sum(kts))
        graw = _dot(q_hi, ks_hi) + _dot(q_lo, ks_hi) + _dot(q_hi, ks_lo)
        g_sc[c * nblk_step + jj] = jnp.sum(graw, axis=1, keepdims=True) * (1.0 / MOBA_BLOCK)
    probs = []
    for jj in range(nblk_step):
        s = scores[jj]
        m = jnp.max(s, axis=1, keepdims=True)
        p = jnp.exp(s - m)
        m_sc[c * nblk_step + jj] = m
        l_sc[c * nblk_step + jj] = jnp.sum(p, axis=1, keepdims=True)
        probs.append(p.astype(BF16))
    for jj in range(nblk_step):
        vt = jnp.concatenate([flat(x) for x in v_pages[jj * per_blk:(jj + 1) * per_blk]], axis=1).astype(BF16)
        o_sc[c * nblk_step + jj] = _dot_nt(probs[jj], vt)

    @pl.when(c == pl.num_programs(1) - 1)
    def _():
        gates = g_sc[...]
        blk3 = lax.broadcasted_iota(jnp.int32, gates.shape, 0).astype(F32)
        sel = _top3_rows(gates, blk3 >= 0.0, blk3) > 0.0
        m_all = m_sc[...]
        kn = kn_ref[...]
        vn = vn_ref[...]
        t_idx = lax.broadcasted_iota(jnp.int32, (rows, 1), 0) % tq
        s_own = []
        m_own = jnp.full((rows, 1), NEG_INF, F32)
        for kk in range(tq):
            sk = jnp.sum(qx * (HEAD_DIM ** -0.5) * kn[kk:kk + 1, :], axis=1, keepdims=True)
            sk = jnp.where(t_idx >= kk, sk, NEG_INF)
            s_own.append(sk)
            m_own = jnp.maximum(m_own, sk)
        m_fin = jnp.maximum(m_own, jnp.max(jnp.where(sel, m_all, NEG_INF), axis=0))
        w = jnp.exp(jnp.where(sel, m_all - m_fin, NEG_INF))
        l_fin = jnp.sum(w * l_sc[...], axis=0)
        o_fin = jnp.sum(w * o_sc[...], axis=0)
        for kk in range(tq):
            pk = jnp.exp(s_own[kk] - m_fin)
            l_fin = l_fin + pk
            o_fin = o_fin + pk * vn[kk:kk + 1, :]
        o_ref[...] = o_fin / l_fin


def _moba_sample(q, k_new, v_new, cache_k, cache_v, layer, page_table):
    bs, tq, a = q.shape
    page, nheads = cache_k.shape[2], cache_k.shape[3]
    n_pages = page_table.shape[1]
    past = n_pages * page
    assert cache_k.shape[4] == HEAD_DIM and a == nheads * HEAD_DIM and page % LANES == 0
    assert MOBA_BLOCK % page == 0 and past % MOBA_BLOCK == 0 and tq <= MOBA_BLOCK
    per_blk = MOBA_BLOCK // page
    nb = past // MOBA_BLOCK
    assert nb >= MOBA_TOPK
    pps = per_blk * 4
    while n_pages % pps:
        pps -= per_blk
    rows = nheads * tq
    qrep = jnp.tile(q, (1, nheads, 1))
    ck_t = cache_k.transpose(0, 1, 3, 4, 2)
    cv_t = cache_v.transpose(0, 1, 3, 4, 2)

    def page_spec(r):
        return pl.BlockSpec((None, None, nheads, HEAD_DIM, page),
                            lambda b, c, pt: (layer, pt[b, c * pps + r], 0, 0, 0))

    seq_spec = pl.BlockSpec((None, tq, a), lambda b, c, pt: (b, 0, 0))
    row_spec = pl.BlockSpec((None, rows, a), lambda b, c, pt: (b, 0, 0))
    o_full = pl.pallas_call(
        functools.partial(_moba_sample_kernel, pps=pps, tq=tq, nheads=nheads),
        grid_spec=pltpu.PrefetchScalarGridSpec(
            num_scalar_prefetch=1,
            grid=(bs, n_pages // pps),
            in_specs=[row_spec, seq_spec, seq_spec] + [page_spec(r) for r in range(pps)] * 2,
            out_specs=row_spec,
            scratch_shapes=[
                pltpu.VMEM((nb, rows, 1), F32),
                pltpu.VMEM((nb, rows, 1), F32),
                pltpu.VMEM((nb, rows, 1), F32),
                pltpu.VMEM((nb, rows, a), F32),
            ],
        ),
        out_shape=jax.ShapeDtypeStruct((bs, rows, a), F32),
        compiler_params=_params(("parallel", "arbitrary")),
        name="moba_sample",
    )(page_table, qrep, k_new, v_new, *([ck_t] * pps), *([cv_t] * pps))
    o5 = o_full.reshape(bs, nheads, tq, nheads, HEAD_DIM)
    hidx = jnp.arange(nheads)
    return o5[:, hidx, :, hidx, :].transpose(1, 2, 0, 3).reshape(bs, tq, a)


def _ssd_prompt_kernel(xbc_ref, dt_ref, cw_ref, cb_ref, dtb_ref, alog_ref, dsk_ref, y_ref, hout_ref,
                       xwin_sc, h_sc, *, d_inner, nheads):
    cl = SSD_CHUNK
    c = pl.program_id(1)

    @pl.when(c == 0)
    def _():
        xwin_sc[0:SUBLANES, :] = jnp.zeros((SUBLANES, xwin_sc.shape[1]), F32)
        h_sc[...] = jnp.zeros(h_sc.shape, F32)

    cur = xbc_ref[...]
    xwin_sc[SUBLANES:SUBLANES + cl, :] = cur
    acc = jnp.broadcast_to(cb_ref[...], cur.shape)
    for i in range(CONV_W):
        acc = acc + xwin_sc[pl.ds(SUBLANES - (CONV_W - 1) + i, cl), :] * cw_ref[i:i + 1, :]
    xwin_sc[0:SUBLANES, :] = cur[cl - SUBLANES:cl, :]
    xc = _silu(acc)
    gn = SSM_GROUPS * D_STATE
    xs = xc[:, :d_inner]
    bm = xc[:, d_inner:d_inner + gn]
    cm = xc[:, d_inner + gn:d_inner + 2 * gn]

    dt = _softplus(dt_ref[...] + dtb_ref[...])
    da = dt * (-jnp.exp(alog_ref[...]))
    li = lax.broadcasted_iota(jnp.int32, (cl, cl), 0)
    si = lax.broadcasted_iota(jnp.int32, (cl, cl), 1)
    causal = li >= si
    tril = jnp.where(causal, 1.0, 0.0).astype(BF16)
    d1, d2, d3 = _split3(da)
    acs = _dot(tril, d1) + _dot(tril, d2) + _dot(tril, d3)
    acs_t = acs.T
    dt_t = dt.T
    xs_t = xs.T
    lane = lax.broadcasted_iota(jnp.int32, (1, LANES), 1)
    lo_half = lane < SSM_HEAD_DIM
    hpg = nheads // SSM_GROUPS
    ys = []
    for hp in range(nheads // 2):
        g = (2 * hp) // hpg
        bg = bm[:, g * D_STATE:(g + 1) * D_STATE]
        cg = cm[:, g * D_STATE:(g + 1) * D_STATE].astype(BF16)
        cb = _dot_nt(cg, bg.astype(BF16))
        x2 = xs[:, hp * LANES:(hp + 1) * LANES]
        x2b = x2.astype(BF16)
        yd = []
        ecol = []
        for hh in range(2):
            h = 2 * hp + hh
            acol = acs[:, h:h + 1]
            diff = acol - acs_t[h:h + 1, :]
            lm = jnp.exp(jnp.where(causal, diff, NEG_INF))
            mh = cb * lm * dt_t[h:h + 1, :]
            yd.append(_dot(mh.astype(BF16), x2b))
            ecol.append(jnp.exp(acol))
            alast = acs[cl - 1:cl, h:h + 1]
            wcol = jnp.exp(alast - acol) * dt[:, h:h + 1]
            st = _dot(xs_t[h * SSM_HEAD_DIM:(h + 1) * SSM_HEAD_DIM, :].astype(BF16), (bg * wcol).astype(BF16))
            hprev = h_sc[h]
            if hh == 0:
                hprev0 = hprev
            else:
                hprev2 = jnp.concatenate([hprev0, hprev], axis=0).astype(BF16)
            h_sc[h] = jnp.exp(alast) * hprev + st
        yo = _dot_nt(cg, hprev2) * jnp.where(lo_half, ecol[0], ecol[1])
        dsk2 = jnp.where(lo_half, dsk_ref[:, 2 * hp:2 * hp + 1], dsk_ref[:, 2 * hp + 1:2 * hp + 2])
        ys.append(jnp.where(lo_half, yd[0], yd[1]) + yo + dsk2 * x2)
    y_ref[...] = jnp.concatenate(ys, axis=1)

    @pl.when(c == pl.num_programs(1) - 1)
    def _():
        hout_ref[...] = h_sc[...]


def _ssd_prompt(xbc, dtp, cw, cb, dtb, alog, dsk, d_inner, nheads):
    bsz, t, cdim = xbc.shape
    assert t % SSD_CHUNK == 0 and nheads % 2 == 0 and (nheads // SSM_GROUPS) % 2 == 0
    nc = t // SSD_CHUNK
    small = lambda shape: pl.BlockSpec(shape, lambda b, c: (0, 0))
    return pl.pallas_call(
        functools.partial(_ssd_prompt_kernel, d_inner=d_inner, nheads=nheads),
        grid=(bsz, nc),
        in_specs=[
            pl.BlockSpec((None, SSD_CHUNK, cdim), lambda b, c: (b, c, 0)),
            pl.BlockSpec((None, SSD_CHUNK, LANES), lambda b, c: (b, c, 0)),
            small(cw.shape), small(cb.shape), small(dtb.shape), small(alog.shape), small(dsk.shape),
        ],
        out_specs=[
            pl.BlockSpec((None, SSD_CHUNK, d_inner), lambda b, c: (b, c, 0)),
            pl.BlockSpec((None, nheads, SSM_HEAD_DIM, D_STATE), lambda b, c: (b, 0, 0, 0)),
        ],
        out_shape=[
            jax.ShapeDtypeStruct((bsz, t, d_inner), F32),
            jax.ShapeDtypeStruct((bsz, nheads, SSM_HEAD_DIM, D_STATE), F32),
        ],
        scratch_shapes=[
            pltpu.VMEM((SUBLANES + SSD_CHUNK, cdim), F32),
            pltpu.VMEM((nheads, SSM_HEAD_DIM, D_STATE), F32),
        ],
        compiler_params=_params(("parallel", "arbitrary")),
        name="ssd_prompt",
    )(xbc, dtp, cw, cb, dtb, alog, dsk)


def _ssd_sample_kernel(xf_ref, dtx_ref, h0_ref, cw_ref, cb_ref, dtbx_ref, alogx_ref, dskx_ref, y_ref, hout_ref,
                       rows_sc, cpad_sc, bpad_sc, *, t, d_inner, nheads):
    cdim = xf_ref.shape[1]
    acc = jnp.broadcast_to(cb_ref[...], (t, cdim))
    for i in range(CONV_W):
        acc = acc + xf_ref[pl.ds(i, t), :] * cw_ref[i:i + 1, :]
    xc = _silu(acc)
    gn = SSM_GROUPS * D_STATE
    xs = xc[:, :d_inner]
    bm = xc[:, d_inner:d_inner + gn]
    cm = xc[:, d_inner + gn:d_inner + 2 * gn]
    dt = _softplus(dtx_ref[...] + dtbx_ref[...])
    da = dt * (-jnp.exp(alogx_ref[...]))
    acs = [da[0:1, :]]
    for s in range(1, t):
        acs.append(acs[-1] + da[s:s + 1, :])
    lane = lax.broadcasted_iota(jnp.int32, (1, d_inner), 1)
    grp = lane // (d_inner // SSM_GROUPS)

    rows_sc[...] = jnp.zeros(rows_sc.shape, F32)
    for s in range(t):
        rows_sc[s:s + 1, :] = jnp.exp(acs[t - 1] - acs[s]) * dt[s:s + 1, :] * xs[s:s + 1, :]
    rows_sc[t:t + 1, :] = jnp.exp(acs[t - 1])
    cols = rows_sc[...].T
    cpad_sc[...] = jnp.zeros(cpad_sc.shape, F32)
    bpad_sc[...] = jnp.zeros(bpad_sc.shape, F32)
    for g in range(SSM_GROUPS):
        cpad_sc[g, 0:t, :] = cm[:, g * D_STATE:(g + 1) * D_STATE]
        bpad_sc[g, 0:t, :] = bm[:, g * D_STATE:(g + 1) * D_STATE]

    h0 = h0_ref[...].reshape(d_inner, D_STATE)
    half = d_inner // SSM_GROUPS
    yoff_t = []
    hnew = []
    for g in range(SSM_GROUPS):
        h0g = h0[g * half:(g + 1) * half, :]
        yoff_t.append(_dot_nt(h0g.astype(BF16), cpad_sc[g].astype(BF16)))
        colg = cols[g * half:(g + 1) * half, :]
        w_hi, w_lo = _split2(jnp.where(lax.broadcasted_iota(jnp.int32, colg.shape, 1) < t, colg, 0.0))
        b_hi, b_lo = _split2(bpad_sc[g])
        upd = _dot(w_hi, b_hi) + _dot(w_hi, b_lo) + _dot(w_lo, b_hi)
        hnew.append(h0g * colg[:, t:t + 1] + upd)
    hout_ref[...] = jnp.concatenate(hnew, axis=0).reshape(hout_ref.shape)
    yoff = jnp.concatenate(yoff_t, axis=0).T

    for tt in range(t):
        y = yoff[tt:tt + 1, :] * jnp.exp(acs[tt]) + dskx_ref[...] * xs[tt:tt + 1, :]
        for s in range(tt + 1):
            cbs = []
            for g in range(SSM_GROUPS):
                sl = slice(g * D_STATE, (g + 1) * D_STATE)
                cbs.append(jnp.sum(cm[tt:tt + 1, sl] * bm[s:s + 1, sl], axis=1, keepdims=True))
            cbx = jnp.where(grp == 0, cbs[0], cbs[1])
            y = y + cbx * jnp.exp(acs[tt] - acs[s]) * dt[s:s + 1, :] * xs[s:s + 1, :]
        y_ref[tt:tt + 1, :] = y


def _ssd_sample(xf, dtx, h0, cw, cb, dtbx, alogx, dskx, t, d_inner, nheads):
    bs, tf, cdim = xf.shape
    assert SSM_GROUPS == 2 and t + 1 <= LANES
    small = lambda shape: pl.BlockSpec(shape, lambda b: (0, 0))
    return pl.pallas_call(
        functools.partial(_ssd_sample_kernel, t=t, d_inner=d_inner, nheads=nheads),
        grid=(bs,),
        in_specs=[
            pl.BlockSpec((None, tf, cdim), lambda b: (b, 0, 0)),
            pl.BlockSpec((None, t, d_inner), lambda b: (b, 0, 0)),
            pl.BlockSpec((None, nheads, SSM_HEAD_DIM, D_STATE), lambda b: (b, 0, 0, 0)),
            small(cw.shape), small(cb.shape), small(dtbx.shape), small(alogx.shape), small(dskx.shape),
        ],
        out_specs=[
            pl.BlockSpec((None, t, d_inner), lambda b: (b, 0, 0)),
            pl.BlockSpec((None, nheads, SSM_HEAD_DIM, D_STATE), lambda b: (b, 0, 0, 0)),
        ],
        out_shape=[
            jax.ShapeDtypeStruct((bs, t, d_inner), F32),
            jax.ShapeDtypeStruct((bs, nheads, SSM_HEAD_DIM, D_STATE), F32),
        ],
        scratch_shapes=[
            pltpu.VMEM((LANES, d_inner), F32),
            pltpu.VMEM((SSM_GROUPS, LANES, D_STATE), F32),
            pltpu.VMEM((SSM_GROUPS, LANES, D_STATE), F32),
        ],
        compiler_params=_params(("parallel",)),
        name="ssd_sample",
    )(xf, dtx, h0, cw, cb, dtbx, alogx, dskx)


def _merge_kernel(o_ref, za_ref, y_ref, zm_ref, g_ref, x_ref, p_ref, wa_ref, ws_ref, wo_ref, wpg_ref, wp_ref,
                  snw_ref, pnw_ref, fnw_ref, out_ref, *, final):
    d = x_ref.shape[1]
    ya = _dot((o_ref[...] * _silu(za_ref[...])).astype(BF16), wa_ref[...])
    yz = y_ref[...] * _silu(zm_ref[...])
    gw = yz.shape[1] // SSM_GROUPS
    parts = []
    for g in range(SSM_GROUPS):
        seg = yz[:, g * gw:(g + 1) * gw]
        parts.append(seg * lax.rsqrt(jnp.mean(seg * seg, axis=-1, keepdims=True) + EPS))
    yn = jnp.concatenate(parts, axis=1) * snw_ref[...]
    ys = _dot(yn.astype(BF16), ws_ref[...])
    mix = _sigmoid(g_ref[:, :d]) * ya + _sigmoid(g_ref[:, d:]) * ys
    h1 = x_ref[...] + _dot(mix.astype(BF16), wo_ref[...])
    pg = _sigmoid(_dot(_rms(h1, pnw_ref[...]).astype(BF16), wpg_ref[...]))
    h2 = h1 + _dot(p_ref[...].astype(BF16), wp_ref[...]) * pg
    out_ref[...] = _rms(h2, fnw_ref[...]) if final else h2


def _merge(o, za, y, zm, gl, x, p, weights, norms, tm, final):
    n, d = x.shape
    assert n % tm == 0
    row = lambda arr: pl.BlockSpec((tm, arr.shape[1]), lambda i: (i, 0))
    whole = lambda arr: pl.BlockSpec(arr.shape, lambda i: (0, 0))
    acts = [o, za, y, zm, gl, x, p]
    return pl.pallas_call(
        functools.partial(_merge_kernel, final=final),
        grid=(n // tm,),
        in_specs=[row(a) for a in acts] + [whole(w) for w in weights] + [whole(w) for w in norms],
        out_specs=pl.BlockSpec((tm, d), lambda i: (i, 0)),
        out_shape=jax.ShapeDtypeStruct((n, d), F32),
        compiler_params=_params(("parallel",)),
        name="merge_out",
    )(*acts, *weights, *norms)


def _layer(h, p_l, q_start, past, conv_state, ssm_state, lw, final_norm_w, final):
    (norm_mix_w, w_in, conv_w, conv_b, dt_bias, a_log, d_skip, ssm_norm_w,
     w_attn_br, w_ssm_br, w_out, ple_norm_w, w_ple_gate, w_ple) = lw
    bsz, t, d = h.shape
    a = w_attn_br.shape[0]
    d_inner = w_ssm_br.shape[0]
    nheads = dt_bias.shape[0]
    cdim = conv_w.shape[1]
    n = bsz * t
    row = lambda v: v.reshape(1, -1).astype(F32)

    sizes = (a, a, a, a, d_inner, cdim, nheads, 2 * d)
    offs = [0]
    for s in sizes:
        offs.append(offs[-1] + s)
    cols = [w_in[:, offs[i]:offs[i + 1]] for i in range(len(sizes))]
    cols[6] = jnp.pad(cols[6], ((0, 0), (0, LANES - nheads)))
    widths = (a, a, a, a, d_inner, cdim, LANES, 2 * d)
    w_pad = jnp.concatenate(cols, axis=1).astype(BF16)

    cos, sin = _rope_tables(t, q_start)
    prompt = past is None
    if prompt:
        tm = MOBA_BLOCK
        tiles_per_seq = t // tm
        cos_map = lambda i: (i % tiles_per_seq, 0)
    else:
        tm = n if n <= 256 else 256
        assert tm % t == 0
        cos = jnp.tile(cos, (tm // t, 1))
        sin = jnp.tile(sin, (tm // t, 1))
        cos_map = lambda i: (0, 0)
    outs = _inproj(h.reshape(n, d), row(norm_mix_w), w_pad, cos, sin, widths, tm, cos_map, prompt)
    q, k, v, za, zm, xbc, dtp, gl = outs[:8]
    q3, k3, v3 = (z.reshape(bsz, t, a) for z in (q, k, v))

    cw = conv_w.astype(F32)
    cb = row(conv_b)
    if prompt:
        kmean = outs[8].reshape(bsz, t // MOBA_BLOCK, a)
        o = _moba_prompt(q3, k3, v3, kmean)
        pad = lambda vec: jnp.pad(row(vec), ((0, 0), (0, LANES - nheads)))
        y_ssd, ssm_new = _ssd_prompt(xbc.reshape(bsz, t, cdim), dtp.reshape(bsz, t, LANES), cw, cb,
                                     pad(dt_bias), pad(a_log), pad(d_skip), d_inner, nheads)
        xbc_full_tail = xbc.reshape(bsz, t, cdim)[:, t - (CONV_W - 1):, :]
        if t < CONV_W - 1:
            raise NotImplementedError("prompt shorter than the conv window")
        conv_new = xbc_full_tail
    else:
        cache_k, cache_v, layer, page_table = past
        o = _moba_sample(q3, k3, v3, cache_k, cache_v, layer, page_table)
        xf = jnp.concatenate([conv_state.astype(F32), xbc.reshape(bsz, t, cdim)], axis=1)
        conv_new = xf[:, -(CONV_W - 1):, :]
        rep = lambda vec: jnp.repeat(row(vec), SSM_HEAD_DIM, axis=1)
        dtx = jnp.repeat(dtp.reshape(bsz, t, LANES)[:, :, :nheads], SSM_HEAD_DIM, axis=2)
        y_ssd, ssm_new = _ssd_sample(xf, dtx, ssm_state.astype(F32), cw, cb, rep(dt_bias), rep(a_log),
                                     rep(d_skip), t, d_inner, nheads)

    weights = [w.astype(BF16) for w in (w_attn_br, w_ssm_br, w_out, w_ple_gate, w_ple)]
    norms = [row(ssm_norm_w), row(ple_norm_w), row(final_norm_w)]
    tm_out = 256 if n % 256 == 0 else n
    h_out = _merge(o.reshape(n, a), za, y_ssd.reshape(n, d_inner), zm, gl, h.reshape(n, d),
                   p_l.reshape(n, -1).astype(F32), weights, norms, tm_out, final)
    nh_attn = a // HEAD_DIM
    return (h_out.reshape(bsz, t, d), k3.reshape(bsz, t, nh_attn, HEAD_DIM), v3.reshape(bsz, t, nh_attn, HEAD_DIM),
            conv_new, ssm_new)


def kernel(x_prompt, x_sample, cache_k, cache_v, state_conv, state_ssm, page_table, p_prompt, p_sample,
           norm_mix_w, w_in, conv_w, conv_b, dt_bias, a_log, d_skip, ssm_norm_w,
           w_attn_br, w_ssm_br, w_out, ple_norm_w, w_ple_gate, w_ple, final_norm_w):
    depth = w_in.shape[0]
    past_len = page_table.shape[1] * cache_k.shape[2]
    h_p, h_s = x_prompt, x_sample
    acc = [[] for _ in range(8)]
    for l in range(depth):
        lw = (norm_mix_w[l], w_in[l], conv_w[l], conv_b[l], dt_bias[l], a_log[l], d_skip[l], ssm_norm_w[l],
              w_attn_br[l], w_ssm_br[l], w_out[l], ple_norm_w[l], w_ple_gate[l], w_ple[l])
        final = l == depth - 1
        h_p, kp, vp, cp, sp = _layer(h_p, p_prompt[l], 0, None, None, None, lw, final_norm_w, final)
        h_s, ks, vs, cs, ss = _layer(h_s, p_sample[l], past_len, (cache_k, cache_v, l, page_table),
                                     state_conv[l], state_ssm[l], lw, final_norm_w, final)
        for lst, val in zip(acc, (kp, vp, cp, sp, ks, vs, cs, ss)):
            lst.append(val)
    return (h_p, h_s) + tuple(jnp.stack(lst) for lst in acc)
```

```python
import functools

import jax
import jax.numpy as jnp
from jax import lax
from jax.experimental import pallas as pl
from jax.experimental.pallas import tpu as pltpu

F32 = jnp.float32
BF16 = jnp.bfloat16

HEAD_DIM = 64
MOBA_BLOCK = 256
MOBA_TOPK = 3
ROPE_THETA = 10000.0
SSM_HEAD_DIM = 64
SSM_GROUPS = 2
D_STATE = 128
CONV_W = 4
SSD_CHUNK = 128
EPS = 1e-6
NEG_INF = -1e30

LANES = 128
SUBLANES = 8
VMEM_LIMIT_BYTES = 56 * 1024 * 1024


def _sigmoid(x):
    return 1.0 / (1.0 + jnp.exp(-x))


def _silu(x):
    return x * _sigmoid(x)


def _softplus(x):
    return jnp.maximum(x, 0.0) + jnp.log1p(jnp.exp(-jnp.abs(x)))


def _rms(x, w):
    ms = jnp.mean(x * x, axis=-1, keepdims=True)
    return x * lax.rsqrt(ms + EPS) * w


def _dot(a, b):
    return jnp.dot(a, b, preferred_element_type=F32)


def _dot_nt(a, b):
    return lax.dot_general(a, b, (((1,), (1,)), ((), ())), preferred_element_type=F32)


def _split2(x):
    hi = x.astype(BF16)
    lo = (x - hi.astype(F32)).astype(BF16)
    return hi, lo


def _split3(x):
    b1 = x.astype(BF16)
    r1 = x - b1.astype(F32)
    b2 = r1.astype(BF16)
    b3 = (r1 - b2.astype(F32)).astype(BF16)
    return b1, b2, b3


def _params(sem):
    return pltpu.CompilerParams(dimension_semantics=sem, vmem_limit_bytes=VMEM_LIMIT_BYTES)


def _rope_table_kernel(inv_ref, cos_ref, sin_ref, *, start, rows):
    i = pl.program_id(0)
    pos = lax.broadcasted_iota(jnp.int32, (rows, LANES), 0) + (start + i * rows)
    lane = lax.broadcasted_iota(jnp.int32, (rows, LANES), 1)
    ang = pos.astype(F32) * inv_ref[...]
    first_half = (lane % HEAD_DIM) < (HEAD_DIM // 2)
    cos_ref[...] = jnp.cos(ang)
    s = jnp.sin(ang)
    sin_ref[...] = jnp.where(first_half, -s, s)


def _rope_tables(t, start):
    half = HEAD_DIM // 2
    inv = ROPE_THETA ** (-jnp.arange(half, dtype=F32) / half)
    inv = jnp.tile(inv, LANES // half)[None, :]
    rows = min(t, 512)
    assert t % rows == 0
    spec = pl.BlockSpec((rows, LANES), lambda i: (i, 0))
    return pl.pallas_call(
        functools.partial(_rope_table_kernel, start=start, rows=rows),
        grid=(t // rows,),
        in_specs=[pl.BlockSpec((1, LANES), lambda i: (0, 0))],
        out_specs=[spec, spec],
        out_shape=[jax.ShapeDtypeStruct((t, LANES), F32)] * 2,
        compiler_params=_params(("parallel",)),
        name="rope_tables",
    )(inv)


def _rope(x, cos, sin_signed):
    lane = lax.broadcasted_iota(jnp.int32, cos.shape, 1)
    first_half = (lane % HEAD_DIM) < (HEAD_DIM // 2)
    half = HEAD_DIM // 2
    outs = []
    for c in range(x.shape[1] // LANES):
        xs = x[:, c * LANES:(c + 1) * LANES]
        partner = jnp.where(first_half, pltpu.roll(xs, LANES - half, 1), pltpu.roll(xs, half, 1))
        outs.append(xs * cos + partner * sin_signed)
    return jnp.concatenate(outs, axis=1)


KV_GROUP = 2
GROUPS_PER_TRIP = 4
VAUG_ROWS = 80


def _inproj_kernel(x_ref, nw_ref, w_ref, cos_ref, sin_ref, *out_refs, widths, prompt, tiles_per_seq):
    u = _rms(x_ref[...], nw_ref[...]).astype(BF16)
    cos = cos_ref[...]
    sin = sin_ref[...]
    offs = [0]
    for wd in widths:
        offs.append(offs[-1] + wd)
    proj = lambda idx: _dot(u, w_ref[:, offs[idx]:offs[idx + 1]])
    q = _rope(proj(0), cos, sin)
    k = _rope(proj(1), cos, sin)
    v = proj(2)
    if not prompt:
        dense = dict(enumerate(out_refs))
        dense[0][...] = q
        dense[1][...] = k
        dense[2][...] = v
        first_dense = 3
    else:
        q_ref, kt_ref, vt_ref, kaug_ref, vaug_ref, kmean_ref = out_refs[:6]
        dense = {3 + j: r for j, r in enumerate(out_refs[6:])}
        first_dense = 3
        tm = q.shape[0]
        half = LANES // 2
        blk = pl.program_id(0) % tiles_per_seq
        q_ref[...] = q
        kmean_ref[...] = jnp.mean(k, axis=0, keepdims=True)
        kt_ref[...] = k.T
        v_t = v.T
        vt_ref[...] = v_t
        lane = lax.broadcasted_iota(jnp.int32, (1, LANES), 1)
        oh_hi = jnp.where(lane == half + blk, 1.0, 0.0)
        oh_lo = jnp.where(lane == blk, 1.0, 0.0)
        for p in range(k.shape[1] // LANES):
            k2 = k[:, p * LANES:(p + 1) * LANES]
            left = jnp.where(lane < half, k2, oh_hi)
            right = jnp.where(lane >= half, k2, oh_lo)
            kaug_ref[:, 2 * p * LANES:(2 * p + 2) * LANES] = jnp.concatenate([left, right], axis=1).astype(BF16)
        tail_row = lax.broadcasted_iota(jnp.int32, (VAUG_ROWS - HEAD_DIM, tm), 0)
        tail = jnp.where(tail_row == 0, 1.0, 0.0).astype(BF16)
        for h in range(v.shape[1] // HEAD_DIM):
            vh = v_t[h * HEAD_DIM:(h + 1) * HEAD_DIM, :].astype(BF16)
            vaug_ref[h * VAUG_ROWS:(h + 1) * VAUG_ROWS, :] = jnp.concatenate([vh, tail], axis=0)
    for idx in range(first_dense, len(widths)):
        dense[idx][...] = proj(idx)


def _inproj(x2d, nw, w_pad, cos, sin, widths, tm, cos_map, seq_len):
    n, d = x2d.shape
    assert n % tm == 0
    prompt = seq_len is not None
    rows = lambda wd: (jax.ShapeDtypeStruct((n, wd), F32), pl.BlockSpec((tm, wd), lambda i: (i, 0)))
    outs = [rows(wd) for wd in widths]
    tiles_per_seq = 1
    if prompt:
        assert tm == MOBA_BLOCK and seq_len % (KV_GROUP * tm) == 0 and seq_len // tm <= LANES // 2
        tiles_per_seq = seq_len // tm
        bsz = n // seq_len
        a = widths[1]
        nheads = a // HEAD_DIM
        tmap = lambda i: (i // tiles_per_seq, 0, i % tiles_per_seq)
        outs = [
            rows(widths[0]),
            (jax.ShapeDtypeStruct((bsz, a, seq_len), F32), pl.BlockSpec((None, a, tm), tmap)),
            (jax.ShapeDtypeStruct((bsz, a, seq_len), F32), pl.BlockSpec((None, a, tm), tmap)),
            (jax.ShapeDtypeStruct((n, 2 * a), BF16), pl.BlockSpec((tm, 2 * a), lambda i: (i, 0))),
            (jax.ShapeDtypeStruct((bsz, tiles_per_seq // KV_GROUP, nheads * VAUG_ROWS, KV_GROUP * tm), BF16),
             pl.BlockSpec((None, None, nheads * VAUG_ROWS, tm),
                          lambda i: (i // tiles_per_seq, (i % tiles_per_seq) // KV_GROUP, 0, i % KV_GROUP))),
            (jax.ShapeDtypeStruct((n // tm, 1, a), F32), pl.BlockSpec((None, 1, a), lambda i: (i, 0, 0))),
        ] + outs[3:]
    return pl.pallas_call(
        functools.partial(_inproj_kernel, widths=widths, prompt=prompt, tiles_per_seq=tiles_per_seq),
        grid=(n // tm,),
        in_specs=[
            pl.BlockSpec((tm, d), lambda i: (i, 0)),
            pl.BlockSpec((1, d), lambda i: (0, 0)),
            pl.BlockSpec(w_pad.shape, lambda i: (0, 0)),
            pl.BlockSpec((tm, LANES), cos_map),
            pl.BlockSpec((tm, LANES), cos_map),
        ],
        out_specs=[o[1] for o in outs],
        out_shape=[o[0] for o in outs],
        compiler_params=_params(("parallel",)),
        name="inproj",
    )(x2d, nw, w_pad, cos, sin)


LOG2E = 1.4426950408889634


def _top3_rows(gate, valid, blk):
    g = jnp.where(valid, gate, NEG_INF)
    sel = jnp.zeros(g.shape, F32)
    big = jnp.float32(1e9)
    for _ in range(MOBA_TOPK):
        mx = jnp.max(g, axis=0, keepdims=True)
        idx = jnp.min(jnp.where(g == mx, blk, big), axis=0, keepdims=True)
        pick = blk == idx
        sel = jnp.where(pick, 1.0, sel)
        g = jnp.where(pick, -jnp.inf, g)
    return jnp.where(valid, sel, 0.0)


def _moba_prompt_kernel(q_ref, kown_ref, vown_ref, k_ref, vt_ref, km_ref, o_ref, *, nblk):
    tq = MOBA_BLOCK
    half = LANES // 2
    own = pl.program_id(2)
    q = q_ref[...]
    km_hi, km_lo = _split2(km_ref[...])
    lane = lax.broadcasted_iota(jnp.int32, (1, LANES), 1)
    blk = lax.broadcasted_iota(jnp.int32, (nblk, tq), 0).astype(F32)
    zpad = jnp.zeros((half, tq), F32)
    zq = jnp.zeros((tq, LANES), BF16)
    q_plain = []
    q_aug = []
    for hh in range(2):
        mine = (lane // HEAD_DIM) == hh
        qh = jnp.where(mine, q, 0.0)
        q_hi, q_lo = _split2(qh)
        gate = _dot_nt(km_hi, q_hi) + _dot_nt(km_hi, q_lo) + _dot_nt(km_lo, q_hi)
        sel = _top3_rows(gate, blk < own.astype(F32), blk)
        bias = jnp.where(sel > 0.0, 0.0, NEG_INF)
        if nblk < half:
            bias = jnp.concatenate([bias, jnp.zeros((half - nblk, tq), F32)], axis=0)
        bias_t = (jnp.concatenate([zpad, bias], axis=0) if hh == 0 else jnp.concatenate([bias, zpad], axis=0)).T
        qpl = qh * (HEAD_DIM ** -0.5 * LOG2E)
        q_plain.append(qpl.astype(BF16))
        q_aug.append(jnp.where(mine, qpl, bias_t).astype(BF16))
    blockdiag = lambda a, b: jnp.concatenate(
        [jnp.concatenate([a, zq], axis=1), jnp.concatenate([zq, b], axis=1)], axis=0)
    q2_aug = blockdiag(q_aug[0], q_aug[1])

    def pv(vt2, p):
        return jnp.concatenate([_dot(vt2[:VAUG_ROWS], p[:, :tq]), _dot(vt2[VAUG_ROWS:], p[:, tq:])], axis=1)

    kpos = lax.broadcasted_iota(jnp.int32, (tq, 2 * tq), 0)
    qpos = lax.broadcasted_iota(jnp.int32, (tq, 2 * tq), 1) % tq
    s = jnp.where(kpos <= qpos, _dot_nt(kown_ref[...], blockdiag(q_plain[0], q_plain[1])), NEG_INF)
    m0 = jnp.max(s, axis=0, keepdims=True)
    acc0 = pv(vown_ref[...], jnp.exp2(s - m0).astype(BF16))

    rows_g = KV_GROUP * MOBA_BLOCK
    last_g = nblk // KV_GROUP - 1

    def body(it, carry):
        m, acc = carry
        gs = [jnp.minimum(GROUPS_PER_TRIP * it + r, last_g) for r in range(GROUPS_PER_TRIP)]
        ss = [_dot_nt(k_ref[pl.ds(pl.multiple_of(g * rows_g, rows_g), rows_g), :], q2_aug) for g in gs]
        for r, (s, g) in enumerate(zip(ss, gs)):
            if r:
                s = jnp.where(GROUPS_PER_TRIP * it + r <= last_g, s, NEG_INF)
            m_new = jnp.maximum(m, jnp.max(s, axis=0, keepdims=True))
            p = jnp.exp2(s - m_new).astype(BF16)
            acc = jnp.exp2(m - m_new) * acc + pv(vt_ref[g], p)
            m = m_new
        return m, acc

    ngroups = (own + KV_GROUP - 1) // KV_GROUP
    _, acc = lax.fori_loop(0, (ngroups + GROUPS_PER_TRIP - 1) // GROUPS_PER_TRIP, body, (m0, acc0))
    for hh in range(2):
        a_h = acc[:, hh * tq:(hh + 1) * tq]
        o_ref[hh * HEAD_DIM:(hh + 1) * HEAD_DIM, :] = a_h[0:HEAD_DIM, :] / a_h[HEAD_DIM:HEAD_DIM + 1, :]


def _moba_prompt(q, kaug, vaug_t, kmean):
    bsz, t, a = q.shape
    nblk = t // MOBA_BLOCK
    nheads = a // HEAD_DIM
    half = LANES // 2
    assert t % MOBA_BLOCK == 0 and a % LANES == 0 and 2 * HEAD_DIM == LANES
    assert nblk % SUBLANES == 0 and nblk <= half and nblk % KV_GROUP == 0
    return pl.pallas_call(
        functools.partial(_moba_prompt_kernel, nblk=nblk),
        grid=(bsz, nheads // 2, nblk),
        in_specs=[
            pl.BlockSpec((None, MOBA_BLOCK, LANES), lambda b, hp, i: (b, i, hp)),
            pl.BlockSpec((None, MOBA_BLOCK, 2 * LANES), lambda b, hp, i: (b, i, hp)),
            pl.BlockSpec((None, None, 2 * VAUG_ROWS, MOBA_BLOCK), lambda b, hp, i: (b, i // KV_GROUP, hp, i % KV_GROUP)),
            pl.BlockSpec((None, t, 2 * LANES), lambda b, hp, i: (b, 0, hp)),
            pl.BlockSpec((None, nblk // KV_GROUP, 2 * VAUG_ROWS, KV_GROUP * MOBA_BLOCK), lambda b, hp, i: (b, 0, hp, 0)),
            pl.BlockSpec((None, nblk, LANES), lambda b, hp, i: (b, 0, hp)),
        ],
        out_specs=pl.BlockSpec((None, LANES, MOBA_BLOCK), lambda b, hp, i: (b, hp, i)),
        out_shape=jax.ShapeDtypeStruct((bsz, a, t), F32),
        compiler_params=_params(("parallel", "parallel", "arbitrary")),
        name="moba_prompt",
    )(q, kaug, vaug_t, kaug, vaug_t, kmean)


SAMPLE_BLOCKS_PER_STEP = 8

def _moba_sample_kernel(pt_ref, qrep_ref, kn_ref, vn_ref, *refs, pps, tq, nheads):
    del pt_ref
    k_pages = refs[:pps]
    v_pages = refs[pps:2 * pps]
    o_ref = refs[2 * pps]
    g_sc, m_sc, l_sc, o_sc = refs[2 * pps + 1:]
    c = pl.program_id(1)
    rows = nheads * tq
    a = nheads * HEAD_DIM
    page = k_pages[0].shape[-1]
    per_blk = MOBA_BLOCK // page
    nblk_step = pps // per_blk
    row_h = lax.broadcasted_iota(jnp.int32, (rows, a), 0) // tq
    lane_h = lax.broadcasted_iota(jnp.int32, (rows, a), 1) // HEAD_DIM
    qx = jnp.where(row_h == lane_h, qrep_ref[...], 0.0)
    q_hi, q_lo = _split2(qx)
    qs = (qx * (HEAD_DIM ** -0.5)).astype(BF16)
    flat = lambda x: x[...].reshape(a, page)
    scores = []
    for jj in range(nblk_step):
        kts = [flat(x) for x in k_pages[jj * per_blk:(jj + 1) * per_blk]]
        scores.append(_dot(qs, jnp.concatenate(kts, axis=1).astype(BF16)))
        ks_hi, ks_lo = _split2(sum(kts))
        graw = _dot(q_hi, ks_hi) + _dot(q_lo, ks_hi) + _dot(q_hi, ks_lo)
        g_sc[c * nblk_step + jj] = jnp.sum(graw, axis=1, keepdims=True) * (1.0 / MOBA_BLOCK)
    probs = []
    for jj in range(nblk_step):
        s = scores[jj]
        m = jnp.max(s, axis=1, keepdims=True)
        p = jnp.exp(s - m)
        m_sc[c * nblk_step + jj] = m
        l_sc[c * nblk_step + jj] = jnp.sum(p, axis=1, keepdims=True)
        probs.append(p.astype(BF16))
    for jj in range(nblk_step):
        vt = jnp.concatenate([flat(x) for x in v_pages[jj * per_blk:(jj + 1) * per_blk]], axis=1).astype(BF16)
        o_sc[c * nblk_step + jj] = _dot_nt(probs[jj], vt)

    @pl.when(c == pl.num_programs(1) - 1)
    def _():
        gates = g_sc[...]
        blk3 = lax.broadcasted_iota(jnp.int32, gates.shape, 0).astype(F32)
        sel = _top3_rows(gates, blk3 >= 0.0, blk3) > 0.0
        m_all = m_sc[...]
        kn = kn_ref[...]
        vn = vn_ref[...]
        t_idx = lax.broadcasted_iota(jnp.int32, (rows, 1), 0) % tq
        s_own = []
        m_own = jnp.full((rows, 1), NEG_INF, F32)
        for kk in range(tq):
            sk = jnp.sum(qx * (HEAD_DIM ** -0.5) * kn[kk:kk + 1, :], axis=1, keepdims=True)
            sk = jnp.where(t_idx >= kk, sk, NEG_INF)
            s_own.append(sk)
            m_own = jnp.maximum(m_own, sk)
        m_fin = jnp.maximum(m_own, jnp.max(jnp.where(sel, m_all, NEG_INF), axis=0))
        w = jnp.exp(jnp.where(sel, m_all - m_fin, NEG_INF))
        l_fin = jnp.sum(w * l_sc[...], axis=0)
        o_fin = jnp.sum(w * o_sc[...], axis=0)
        for kk in range(tq):
            pk = jnp.exp(s_own[kk] - m_fin)
            l_fin = l_fin + pk
            o_fin = o_fin + pk * vn[kk:kk + 1, :]
        o_ref[...] = o_fin / l_fin


def _moba_sample(q, k_new, v_new, cache_k, cache_v, layer, page_table):
    bs, tq, a = q.shape
    page, nheads = cache_k.shape[2], cache_k.shape[3]
    n_pages = page_table.shape[1]
    past = n_pages * page
    assert cache_k.shape[4] == HEAD_DIM and a == nheads * HEAD_DIM and page % LANES == 0
    assert MOBA_BLOCK % page == 0 and past % MOBA_BLOCK == 0 and tq <= MOBA_BLOCK
    per_blk = MOBA_BLOCK // page
    nb = past // MOBA_BLOCK
    assert nb >= MOBA_TOPK
    pps = per_blk * SAMPLE_BLOCKS_PER_STEP
    while n_pages % pps:
        pps -= per_blk
    rows = nheads * tq
    qrep = jnp.tile(q, (1, nheads, 1))
    ck_t = cache_k.transpose(0, 1, 3, 4, 2)
    cv_t = cache_v.transpose(0, 1, 3, 4, 2)

    def page_spec(r):
        return pl.BlockSpec((None, None, nheads, HEAD_DIM, page),
                            lambda b, c, pt: (layer, pt[b, c * pps + r], 0, 0, 0))

    seq_spec = pl.BlockSpec((None, tq, a), lambda b, c, pt: (b, 0, 0))
    row_spec = pl.BlockSpec((None, rows, a), lambda b, c, pt: (b, 0, 0))
    o_full = pl.pallas_call(
        functools.partial(_moba_sample_kernel, pps=pps, tq=tq, nheads=nheads),
        grid_spec=pltpu.PrefetchScalarGridSpec(
            num_scalar_prefetch=1,
            grid=(bs, n_pages // pps),
            in_specs=[row_spec, seq_spec, seq_spec] + [page_spec(r) for r in range(pps)] * 2,
            out_specs=row_spec,
            scratch_shapes=[
                pltpu.VMEM((nb, rows, 1), F32),
                pltpu.VMEM((nb, rows, 1), F32),
                pltpu.VMEM((nb, rows, 1), F32),
                pltpu.VMEM((nb, rows, a), F32),
            ],
        ),
        out_shape=jax.ShapeDtypeStruct((bs, rows, a), F32),
        compiler_params=_params(("parallel", "arbitrary")),
        name="moba_sample",
    )(page_table, qrep, k_new, v_new, *([ck_t] * pps), *([cv_t] * pps))
    o5 = o_full.reshape(bs, nheads, tq, nheads, HEAD_DIM)
    hidx = jnp.arange(nheads)
    return o5[:, hidx, :, hidx, :].transpose(1, 2, 0, 3).reshape(bs, tq, a)


def _ssd_prompt_kernel(xbc_ref, dt_ref, cw_ref, cb_ref, dtb_ref, alog_ref, dsk_ref, y_ref, hout_ref,
                       xwin_sc, h_sc, *, d_inner, nheads):
    cl = SSD_CHUNK
    c = pl.program_id(1)

    @pl.when(c == 0)
    def _():
        xwin_sc[0:SUBLANES, :] = jnp.zeros((SUBLANES, xwin_sc.shape[1]), F32)
        h_sc[...] = jnp.zeros(h_sc.shape, F32)

    cur = xbc_ref[...]
    xwin_sc[SUBLANES:SUBLANES + cl, :] = cur
    acc = jnp.broadcast_to(cb_ref[...], cur.shape)
    for i in range(CONV_W):
        acc = acc + xwin_sc[pl.ds(SUBLANES - (CONV_W - 1) + i, cl), :] * cw_ref[i:i + 1, :]
    xwin_sc[0:SUBLANES, :] = cur[cl - SUBLANES:cl, :]
    xc = _silu(acc)
    gn = SSM_GROUPS * D_STATE
    xs = xc[:, :d_inner]
    bm = xc[:, d_inner:d_inner + gn]
    cm = xc[:, d_inner + gn:d_inner + 2 * gn]

    dt = _softplus(dt_ref[...] + dtb_ref[...])
    da = dt * (-jnp.exp(alog_ref[...]))
    li = lax.broadcasted_iota(jnp.int32, (cl, cl), 0)
    si = lax.broadcasted_iota(jnp.int32, (cl, cl), 1)
    causal = li >= si
    tril = jnp.where(causal, 1.0, 0.0).astype(BF16)
    d1, d2, d3 = _split3(da)
    acs = _dot(tril, d1) + _dot(tril, d2) + _dot(tril, d3)
    acs_t = acs.T
    dt_t = dt.T
    xs_t = xs.T
    lane = lax.broadcasted_iota(jnp.int32, (1, LANES), 1)
    lo_half = lane < SSM_HEAD_DIM
    hpg = nheads // SSM_GROUPS
    ys = []
    for hp in range(nheads // 2):
        g = (2 * hp) // hpg
        bg = bm[:, g * D_STATE:(g + 1) * D_STATE]
        cg = cm[:, g * D_STATE:(g + 1) * D_STATE].astype(BF16)
        cb = _dot_nt(cg, bg.astype(BF16))
        x2 = xs[:, hp * LANES:(hp + 1) * LANES]
        x2b = x2.astype(BF16)
        yd = []
        ecol = []
        for hh in range(2):
            h = 2 * hp + hh
            acol = acs[:, h:h + 1]
            diff = acol - acs_t[h:h + 1, :]
            lm = jnp.exp(jnp.where(causal, diff, NEG_INF))
            mh = cb * lm * dt_t[h:h + 1, :]
            yd.append(_dot(mh.astype(BF16), x2b))
            ecol.append(jnp.exp(acol))
            alast = acs[cl - 1:cl, h:h + 1]
            wcol = jnp.exp(alast - acol) * dt[:, h:h + 1]
            st = _dot(xs_t[h * SSM_HEAD_DIM:(h + 1) * SSM_HEAD_DIM, :].astype(BF16), (bg * wcol).astype(BF16))
            hprev = h_sc[h]
            if hh == 0:
                hprev0 = hprev
            else:
                hprev2 = jnp.concatenate([hprev0, hprev], axis=0).astype(BF16)
            h_sc[h] = jnp.exp(alast) * hprev + st
        yo = _dot_nt(cg, hprev2) * jnp.where(lo_half, ecol[0], ecol[1])
        dsk2 = jnp.where(lo_half, dsk_ref[:, 2 * hp:2 * hp + 1], dsk_ref[:, 2 * hp + 1:2 * hp + 2])
        ys.append(jnp.where(lo_half, yd[0], yd[1]) + yo + dsk2 * x2)
    y_ref[...] = jnp.concatenate(ys, axis=1)

    @pl.when(c == pl.num_programs(1) - 1)
    def _():
        hout_ref[...] = h_sc[...]


def _ssd_prompt(xbc, dtp, cw, cb, dtb, alog, dsk, d_inner, nheads):
    bsz, t, cdim = xbc.shape
    assert t % SSD_CHUNK == 0 and nheads % 2 == 0 and (nheads // SSM_GROUPS) % 2 == 0
    nc = t // SSD_CHUNK
    small = lambda shape: pl.BlockSpec(shape, lambda b, c: (0, 0))
    return pl.pallas_call(
        functools.partial(_ssd_prompt_kernel, d_inner=d_inner, nheads=nheads),
        grid=(bsz, nc),
        in_specs=[
            pl.BlockSpec((None, SSD_CHUNK, cdim), lambda b, c: (b, c, 0)),
            pl.BlockSpec((None, SSD_CHUNK, LANES), lambda b, c: (b, c, 0)),
            small(cw.shape), small(cb.shape), small(dtb.shape), small(alog.shape), small(dsk.shape),
        ],
        out_specs=[
            pl.BlockSpec((None, SSD_CHUNK, d_inner), lambda b, c: (b, c, 0)),
            pl.BlockSpec((None, nheads, SSM_HEAD_DIM, D_STATE), lambda b, c: (b, 0, 0, 0)),
        ],
        out_shape=[
            jax.ShapeDtypeStruct((bsz, t, d_inner), F32),
            jax.ShapeDtypeStruct((bsz, nheads, SSM_HEAD_DIM, D_STATE), F32),
        ],
        scratch_shapes=[
            pltpu.VMEM((SUBLANES + SSD_CHUNK, cdim), F32),
            pltpu.VMEM((nheads, SSM_HEAD_DIM, D_STATE), F32),
        ],
        compiler_params=_params(("parallel", "arbitrary")),
        name="ssd_prompt",
    )(xbc, dtp, cw, cb, dtb, alog, dsk)


def _ssd_sample_kernel(xf_ref, dtx_ref, h0_ref, cw_ref, cb_ref, dtbx_ref, alogx_ref, dskx_ref, y_ref, hout_ref,
                       rows_sc, cpad_sc, bpad_sc, *, t, d_inner, nheads):
    cdim = xf_ref.shape[1]
    acc = jnp.broadcast_to(cb_ref[...], (t, cdim))
    for i in range(CONV_W):
        acc = acc + xf_ref[pl.ds(i, t), :] * cw_ref[i:i + 1, :]
    xc = _silu(acc)
    gn = SSM_GROUPS * D_STATE
    xs = xc[:, :d_inner]
    bm = xc[:, d_inner:d_inner + gn]
    cm = xc[:, d_inner + gn:d_inner + 2 * gn]
    dt = _softplus(dtx_ref[...] + dtbx_ref[...])
    da = dt * (-jnp.exp(alogx_ref[...]))
    acs = [da[0:1, :]]
    for s in range(1, t):
        acs.append(acs[-1] + da[s:s + 1, :])
    lane = lax.broadcasted_iota(jnp.int32, (1, d_inner), 1)
    grp = lane // (d_inner // SSM_GROUPS)

    rows_sc[...] = jnp.zeros(rows_sc.shape, F32)
    for s in range(t):
        rows_sc[s:s + 1, :] = jnp.exp(acs[t - 1] - acs[s]) * dt[s:s + 1, :] * xs[s:s + 1, :]
    rows_sc[t:t + 1, :] = jnp.exp(acs[t - 1])
    cols = rows_sc[...].T
    cpad_sc[...] = jnp.zeros(cpad_sc.shape, F32)
    bpad_sc[...] = jnp.zeros(bpad_sc.shape, F32)
    for g in range(SSM_GROUPS):
        cpad_sc[g, 0:t, :] = cm[:, g * D_STATE:(g + 1) * D_STATE]
        bpad_sc[g, 0:t, :] = bm[:, g * D_STATE:(g + 1) * D_STATE]

    h0 = h0_ref[...].reshape(d_inner, D_STATE)
    half = d_inner // SSM_GROUPS
    yoff_t = []
    hnew = []
    for g in range(SSM_GROUPS):
        h0g = h0[g * half:(g + 1) * half, :]
        yoff_t.append(_dot_nt(h0g.astype(BF16), cpad_sc[g].astype(BF16)))
        colg = cols[g * half:(g + 1) * half, :]
        w_hi, w_lo = _split2(jnp.where(lax.broadcasted_iota(jnp.int32, colg.shape, 1) < t, colg, 0.0))
        b_hi, b_lo = _split2(bpad_sc[g])
        upd = _dot(w_hi, b_hi) + _dot(w_hi, b_lo) + _dot(w_lo, b_hi)
        hnew.append(h0g * colg[:, t:t + 1] + upd)
    hout_ref[...] = jnp.concatenate(hnew, axis=0).reshape(hout_ref.shape)
    yoff = jnp.concatenate(yoff_t, axis=0).T

    for tt in range(t):
        y = yoff[tt:tt + 1, :] * jnp.exp(acs[tt]) + dskx_ref[...] * xs[tt:tt + 1, :]
        for s in range(tt + 1):
            cbs = []
            for g in range(SSM_GROUPS):
                sl = slice(g * D_STATE, (g + 1) * D_STATE)
                cbs.append(jnp.sum(cm[tt:tt + 1, sl] * bm[s:s + 1, sl], axis=1, keepdims=True))
            cbx = jnp.where(grp == 0, cbs[0], cbs[1])
            y = y + cbx * jnp.exp(acs[tt] - acs[s]) * dt[s:s + 1, :] * xs[s:s + 1, :]
        y_ref[tt:tt + 1, :] = y


def _ssd_sample(xf, dtx, h0, cw, cb, dtbx, alogx, dskx, t, d_inner, nheads):
    bs, tf, cdim = xf.shape
    assert SSM_GROUPS == 2 and t + 1 <= LANES
    small = lambda shape: pl.BlockSpec(shape, lambda b: (0, 0))
    return pl.pallas_call(
        functools.partial(_ssd_sample_kernel, t=t, d_inner=d_inner, nheads=nheads),
        grid=(bs,),
        in_specs=[
            pl.BlockSpec((None, tf, cdim), lambda b: (b, 0, 0)),
            pl.BlockSpec((None, t, d_inner), lambda b: (b, 0, 0)),
            pl.BlockSpec((None, nheads, SSM_HEAD_DIM, D_STATE), lambda b: (b, 0, 0, 0)),
            small(cw.shape), small(cb.shape), small(dtbx.shape), small(alogx.shape), small(dskx.shape),
        ],
        out_specs=[
            pl.BlockSpec((None, t, d_inner), lambda b: (b, 0, 0)),
            pl.BlockSpec((None, nheads, SSM_HEAD_DIM, D_STATE), lambda b: (b, 0, 0, 0)),
        ],
        out_shape=[
            jax.ShapeDtypeStruct((bs, t, d_inner), F32),
            jax.ShapeDtypeStruct((bs, nheads, SSM_HEAD_DIM, D_STATE), F32),
        ],
        scratch_shapes=[
            pltpu.VMEM((LANES, d_inner), F32),
            pltpu.VMEM((SSM_GROUPS, LANES, D_STATE), F32),
            pltpu.VMEM((SSM_GROUPS, LANES, D_STATE), F32),
        ],
        compiler_params=_params(("parallel",)),
        name="ssd_sample",
    )(xf, dtx, h0, cw, cb, dtbx, alogx, dskx)


def _merge_kernel(o_ref, za_ref, y_ref, zm_ref, g_ref, x_ref, p_ref, wa_ref, ws_ref, wo_ref, wpg_ref, wp_ref,
                  snw_ref, pnw_ref, fnw_ref, out_ref, *, final, o_transposed):
    d = x_ref.shape[1]
    o = o_ref[...].T if o_transposed else o_ref[...]
    ya = _dot((o * _silu(za_ref[...])).astype(BF16), wa_ref[...])
    yz = y_ref[...] * _silu(zm_ref[...])
    gw = yz.shape[1] // SSM_GROUPS
    parts = []
    for g in range(SSM_GROUPS):
        seg = yz[:, g * gw:(g + 1) * gw]
        parts.append(seg * lax.rsqrt(jnp.mean(seg * seg, axis=-1, keepdims=True) + EPS))
    yn = jnp.concatenate(parts, axis=1) * snw_ref[...]
    ys = _dot(yn.astype(BF16), ws_ref[...])
    mix = _sigmoid(g_ref[:, :d]) * ya + _sigmoid(g_ref[:, d:]) * ys
    h1 = x_ref[...] + _dot(mix.astype(BF16), wo_ref[...])
    pg = _sigmoid(_dot(_rms(h1, pnw_ref[...]).astype(BF16), wpg_ref[...]))
    h2 = h1 + _dot(p_ref[...].astype(BF16), wp_ref[...]) * pg
    out_ref[...] = _rms(h2, fnw_ref[...]) if final else h2


def _merge(o, za, y, zm, gl, x, p, weights, norms, tm, final):
    n, d = x.shape
    assert n % tm == 0
    row = lambda arr: pl.BlockSpec((tm, arr.shape[1]), lambda i: (i, 0))
    whole = lambda arr: pl.BlockSpec(arr.shape, lambda i: (0, 0))
    o_transposed = o.ndim == 3
    if o_transposed:
        tiles_per_seq = o.shape[2] // tm
        assert o.shape[2] % tm == 0 and o.shape[0] * o.shape[2] == n
        o_spec = pl.BlockSpec((None, o.shape[1], tm), lambda i: (i // tiles_per_seq, 0, i % tiles_per_seq))
    else:
        o_spec = row(o)
    acts = [o, za, y, zm, gl, x, p]
    return pl.pallas_call(
        functools.partial(_merge_kernel, final=final, o_transposed=o_transposed),
        grid=(n // tm,),
        in_specs=[o_spec] + [row(a) for a in acts[1:]] + [whole(w) for w in weights] + [whole(w) for w in norms],
        out_specs=pl.BlockSpec((tm, d), lambda i: (i, 0)),
        out_shape=jax.ShapeDtypeStruct((n, d), F32),
        compiler_params=_params(("parallel",)),
        name="merge_out",
    )(*acts, *weights, *norms)


def _layer(h, p_l, q_start, past, conv_state, ssm_state, lw, final_norm_w, final):
    (norm_mix_w, w_in, conv_w, conv_b, dt_bias, a_log, d_skip, ssm_norm_w,
     w_attn_br, w_ssm_br, w_out, ple_norm_w, w_ple_gate, w_ple) = lw
    bsz, t, d = h.shape
    a = w_attn_br.shape[0]
    d_inner = w_ssm_br.shape[0]
    nheads = dt_bias.shape[0]
    cdim = conv_w.shape[1]
    n = bsz * t
    row = lambda v: v.reshape(1, -1).astype(F32)

    sizes = (a, a, a, a, d_inner, cdim, nheads, 2 * d)
    offs = [0]
    for s in sizes:
        offs.append(offs[-1] + s)
    cols = [w_in[:, offs[i]:offs[i + 1]] for i in range(len(sizes))]
    cols[6] = jnp.pad(cols[6], ((0, 0), (0, LANES - nheads)))
    widths = (a, a, a, a, d_inner, cdim, LANES, 2 * d)
    w_pad = jnp.concatenate(cols, axis=1).astype(BF16)

    cos, sin = _rope_tables(t, q_start)
    prompt = past is None
    if prompt:
        tm = MOBA_BLOCK
        tiles_per_seq = t // tm
        cos_map = lambda i: (i % tiles_per_seq, 0)
    else:
        tm = n if n <= 256 else 256
        assert tm % t == 0
        cos = jnp.tile(cos, (tm // t, 1))
        sin = jnp.tile(sin, (tm // t, 1))
        cos_map = lambda i: (0, 0)
    outs = _inproj(h.reshape(n, d), row(norm_mix_w), w_pad, cos, sin, widths, tm, cos_map, t if prompt else None)
    nh_attn = a // HEAD_DIM
    cw = conv_w.astype(F32)
    cb = row(conv_b)
    if prompt:
        q, k_t, v_t, kaug, vaug_t, kmean, za, zm, xbc, dtp, gl = outs
        o = _moba_prompt(q.reshape(bsz, t, a), kaug.reshape(bsz, t, 2 * a), vaug_t, kmean.reshape(bsz, t // tm, a))
        k_new = k_t.reshape(bsz, nh_attn, HEAD_DIM, t).transpose(0, 3, 1, 2)
        v_new = v_t.reshape(bsz, nh_attn, HEAD_DIM, t).transpose(0, 3, 1, 2)
        pad = lambda vec: jnp.pad(row(vec), ((0, 0), (0, LANES - nheads)))
        y_ssd, ssm_new = _ssd_prompt(xbc.reshape(bsz, t, cdim), dtp.reshape(bsz, t, LANES), cw, cb,
                                     pad(dt_bias), pad(a_log), pad(d_skip), d_inner, nheads)
        if t < CONV_W - 1:
            raise NotImplementedError("prompt shorter than the conv window")
        conv_new = xbc.reshape(bsz, t, cdim)[:, t - (CONV_W - 1):, :]
    else:
        q, k, v, za, zm, xbc, dtp, gl = outs
        q3, k3, v3 = (z.reshape(bsz, t, a) for z in (q, k, v))
        k_new = k3.reshape(bsz, t, nh_attn, HEAD_DIM)
        v_new = v3.reshape(bsz, t, nh_attn, HEAD_DIM)
        cache_k, cache_v, layer, page_table = past
        o = _moba_sample(q3, k3, v3, cache_k, cache_v, layer, page_table).reshape(n, a)
        xf = jnp.concatenate([conv_state.astype(F32), xbc.reshape(bsz, t, cdim)], axis=1)
        conv_new = xf[:, -(CONV_W - 1):, :]
        rep = lambda vec: jnp.repeat(row(vec), SSM_HEAD_DIM, axis=1)
        dtx = jnp.repeat(dtp.reshape(bsz, t, LANES)[:, :, :nheads], SSM_HEAD_DIM, axis=2)
        y_ssd, ssm_new = _ssd_sample(xf, dtx, ssm_state.astype(F32), cw, cb, rep(dt_bias), rep(a_log),
                                     rep(d_skip), t, d_inner, nheads)

    weights = [w.astype(BF16) for w in (w_attn_br, w_ssm_br, w_out, w_ple_gate, w_ple)]
    norms = [row(ssm_norm_w), row(ple_norm_w), row(final_norm_w)]
    tm_out = 256 if n % 256 == 0 else n
    h_out = _merge(o, za, y_ssd.reshape(n, d_inner), zm, gl, h.reshape(n, d),
                   p_l.reshape(n, -1).astype(F32), weights, norms, tm_out, final)
    return h_out.reshape(bsz, t, d), k_new, v_new, conv_new, ssm_new


def kernel(x_prompt, x_sample, cache_k, cache_v, state_conv, state_ssm, page_table, p_prompt, p_sample,
           norm_mix_w, w_in, conv_w, conv_b, dt_bias, a_log, d_skip, ssm_norm_w,
           w_attn_br, w_ssm_br, w_out, ple_norm_w, w_ple_gate, w_ple, final_norm_w):
    depth = w_in.shape[0]
    past_len = page_table.shape[1] * cache_k.shape[2]
    h_p, h_s = x_prompt, x_sample
    acc = [[] for _ in range(8)]
    for l in range(depth):
        lw = (norm_mix_w[l], w_in[l], conv_w[l], conv_b[l], dt_bias[l], a_log[l], d_skip[l], ssm_norm_w[l],
              w_attn_br[l], w_ssm_br[l], w_out[l], ple_norm_w[l], w_ple_gate[l], w_ple[l])
        final = l == depth - 1
        h_p, kp, vp, cp, sp = _layer(h_p, p_prompt[l], 0, None, None, None, lw, final_norm_w, final)
        h_s, ks, vs, cs, ss = _layer(h_s, p_sample[l], past_len, (cache_k, cache_v, l, page_table),
                                     state_conv[l], state_ssm[l], lw, final_norm_w, final)
        for lst, val in zip(acc, (kp, vp, cp, sp, ks, vs, cs, ss)):
            lst.append(val)
    return (h_p, h_s) + tuple(jnp.stack(lst) for lst in acc)
```

```python
import functools

import jax
import jax.numpy as jnp
from jax import lax
from jax.experimental import pallas as pl
from jax.experimental.pallas import tpu as pltpu

F32 = jnp.float32
BF16 = jnp.bfloat16

HEAD_DIM = 64
MOBA_BLOCK = 256
MOBA_TOPK = 3
ROPE_THETA = 10000.0
SSM_HEAD_DIM = 64
SSM_GROUPS = 2
D_STATE = 128
CONV_W = 4
SSD_CHUNK = 128
EPS = 1e-6
NEG_INF = -1e30

LANES = 128
SUBLANES = 8
VMEM_LIMIT_BYTES = 56 * 1024 * 1024


def _sigmoid(x):
    return 0.5 * jnp.tanh(0.5 * x) + 0.5


def _silu(x):
    return x * _sigmoid(x)


def _softplus(x):
    return jnp.maximum(x, 0.0) + jnp.log1p(jnp.exp(-jnp.abs(x)))


def _rms(x, w):
    ms = jnp.mean(x * x, axis=-1, keepdims=True)
    return x * lax.rsqrt(ms + EPS) * w


def _dot(a, b):
    return jnp.dot(a, b, preferred_element_type=F32)


def _dot_nt(a, b):
    return lax.dot_general(a, b, (((1,), (1,)), ((), ())), preferred_element_type=F32)


def _split2(x):
    hi = x.astype(BF16)
    lo = (x - hi.astype(F32)).astype(BF16)
    return hi, lo


def _split3(x):
    b1 = x.astype(BF16)
    r1 = x - b1.astype(F32)
    b2 = r1.astype(BF16)
    b3 = (r1 - b2.astype(F32)).astype(BF16)
    return b1, b2, b3


def _params(sem):
    return pltpu.CompilerParams(dimension_semantics=sem, vmem_limit_bytes=VMEM_LIMIT_BYTES)


def _rope_table_kernel(inv_ref, cos_ref, sin_ref, *, start, rows):
    i = pl.program_id(0)
    pos = lax.broadcasted_iota(jnp.int32, (rows, LANES), 0) + (start + i * rows)
    lane = lax.broadcasted_iota(jnp.int32, (rows, LANES), 1)
    ang = pos.astype(F32) * inv_ref[...]
    first_half = (lane % HEAD_DIM) < (HEAD_DIM // 2)
    cos_ref[...] = jnp.cos(ang)
    s = jnp.sin(ang)
    sin_ref[...] = jnp.where(first_half, -s, s)


def _rope_tables(t, start):
    half = HEAD_DIM // 2
    inv = ROPE_THETA ** (-jnp.arange(half, dtype=F32) / half)
    inv = jnp.tile(inv, LANES // half)[None, :]
    rows = min(t, 512)
    assert t % rows == 0
    spec = pl.BlockSpec((rows, LANES), lambda i: (i, 0))
    return pl.pallas_call(
        functools.partial(_rope_table_kernel, start=start, rows=rows),
        grid=(t // rows,),
        in_specs=[pl.BlockSpec((1, LANES), lambda i: (0, 0))],
        out_specs=[spec, spec],
        out_shape=[jax.ShapeDtypeStruct((t, LANES), F32)] * 2,
        compiler_params=_params(("parallel",)),
        name="rope_tables",
    )(inv)


def _rope(x, cos, sin_signed):
    lane = lax.broadcasted_iota(jnp.int32, cos.shape, 1)
    first_half = (lane % HEAD_DIM) < (HEAD_DIM // 2)
    half = HEAD_DIM // 2
    outs = []
    for c in range(x.shape[1] // LANES):
        xs = x[:, c * LANES:(c + 1) * LANES]
        partner = jnp.where(first_half, pltpu.roll(xs, LANES - half, 1), pltpu.roll(xs, half, 1))
        outs.append(xs * cos + partner * sin_signed)
    return jnp.concatenate(outs, axis=1)


KV_GROUP = 2
GROUPS_PER_TRIP = 4
VAUG_ROWS = 80


def _inproj_kernel(x_ref, nw_ref, w_ref, cos_ref, sin_ref, *out_refs, widths, prompt, tiles_per_seq):
    u = _rms(x_ref[...], nw_ref[...]).astype(BF16)
    cos = cos_ref[...]
    sin = sin_ref[...]
    offs = [0]
    for wd in widths:
        offs.append(offs[-1] + wd)
    proj = lambda idx: _dot(u, w_ref[:, offs[idx]:offs[idx + 1]])
    q = _rope(proj(0), cos, sin)
    k = _rope(proj(1), cos, sin)
    v = proj(2)
    if not prompt:
        dense = dict(enumerate(out_refs))
        dense[0][...] = q
        dense[1][...] = k
        dense[2][...] = v
        first_dense = 3
    else:
        q_ref, kt_ref, vt_ref, kaug_ref, vaug_ref, kmean_ref = out_refs[:6]
        dense = {3 + j: r for j, r in enumerate(out_refs[6:])}
        first_dense = 3
        tm = q.shape[0]
        half = LANES // 2
        blk = pl.program_id(0) % tiles_per_seq
        q_ref[...] = q
        kmean_ref[...] = jnp.mean(k, axis=0, keepdims=True)
        kt_ref[...] = k.T
        v_t = v.T
        vt_ref[...] = v_t
        lane = lax.broadcasted_iota(jnp.int32, (1, LANES), 1)
        oh_hi = jnp.where(lane == half + blk, 1.0, 0.0)
        oh_lo = jnp.where(lane == blk, 1.0, 0.0)
        for p in range(k.shape[1] // LANES):
            k2 = k[:, p * LANES:(p + 1) * LANES]
            left = jnp.where(lane < half, k2, oh_hi)
            right = jnp.where(lane >= half, k2, oh_lo)
            kaug_ref[:, 2 * p * LANES:(2 * p + 2) * LANES] = jnp.concatenate([left, right], axis=1).astype(BF16)
        tail_row = lax.broadcasted_iota(jnp.int32, (VAUG_ROWS - HEAD_DIM, tm), 0)
        tail = jnp.where(tail_row == 0, 1.0, 0.0).astype(BF16)
        for h in range(v.shape[1] // HEAD_DIM):
            vh = v_t[h * HEAD_DIM:(h + 1) * HEAD_DIM, :].astype(BF16)
            vaug_ref[h * VAUG_ROWS:(h + 1) * VAUG_ROWS, :] = jnp.concatenate([vh, tail], axis=0)
    for idx in range(first_dense, len(widths)):
        dense[idx][...] = proj(idx)


def _inproj(x2d, nw, w_pad, cos, sin, widths, tm, cos_map, seq_len):
    n, d = x2d.shape
    assert n % tm == 0
    prompt = seq_len is not None
    rows = lambda wd: (jax.ShapeDtypeStruct((n, wd), F32), pl.BlockSpec((tm, wd), lambda i: (i, 0)))
    outs = [rows(wd) for wd in widths]
    tiles_per_seq = 1
    if prompt:
        assert tm == MOBA_BLOCK and seq_len % (KV_GROUP * tm) == 0 and seq_len // tm <= LANES // 2
        tiles_per_seq = seq_len // tm
        bsz = n // seq_len
        a = widths[1]
        nheads = a // HEAD_DIM
        tmap = lambda i: (i // tiles_per_seq, 0, i % tiles_per_seq)
        outs = [
            rows(widths[0]),
            (jax.ShapeDtypeStruct((bsz, a, seq_len), F32), pl.BlockSpec((None, a, tm), tmap)),
            (jax.ShapeDtypeStruct((bsz, a, seq_len), F32), pl.BlockSpec((None, a, tm), tmap)),
            (jax.ShapeDtypeStruct((n, 2 * a), BF16), pl.BlockSpec((tm, 2 * a), lambda i: (i, 0))),
            (jax.ShapeDtypeStruct((bsz, tiles_per_seq // KV_GROUP, nheads * VAUG_ROWS, KV_GROUP * tm), BF16),
             pl.BlockSpec((None, None, nheads * VAUG_ROWS, tm),
                          lambda i: (i // tiles_per_seq, (i % tiles_per_seq) // KV_GROUP, 0, i % KV_GROUP))),
            (jax.ShapeDtypeStruct((n // tm, 1, a), F32), pl.BlockSpec((None, 1, a), lambda i: (i, 0, 0))),
        ] + outs[3:]
    return pl.pallas_call(
        functools.partial(_inproj_kernel, widths=widths, prompt=prompt, tiles_per_seq=tiles_per_seq),
        grid=(n // tm,),
        in_specs=[
            pl.BlockSpec((tm, d), lambda i: (i, 0)),
            pl.BlockSpec((1, d), lambda i: (0, 0)),
            pl.BlockSpec(w_pad.shape, lambda i: (0, 0)),
            pl.BlockSpec((tm, LANES), cos_map),
            pl.BlockSpec((tm, LANES), cos_map),
        ],
        out_specs=[o[1] for o in outs],
        out_shape=[o[0] for o in outs],
        compiler_params=_params(("parallel",)),
        name="inproj",
    )(x2d, nw, w_pad, cos, sin)


LOG2E = 1.4426950408889634


def _top3_rows(gate, valid, blk):
    g = jnp.where(valid, gate, NEG_INF)
    sel = jnp.zeros(g.shape, F32)
    big = jnp.float32(1e9)
    for _ in range(MOBA_TOPK):
        mx = jnp.max(g, axis=0, keepdims=True)
        idx = jnp.min(jnp.where(g == mx, blk, big), axis=0, keepdims=True)
        pick = blk == idx
        sel = jnp.where(pick, 1.0, sel)
        g = jnp.where(pick, -jnp.inf, g)
    return jnp.where(valid, sel, 0.0)


def _moba_prompt_kernel(q_ref, kown_ref, vown_ref, k_ref, vt_ref, km_ref, o_ref, *, nblk):
    tq = MOBA_BLOCK
    half = LANES // 2
    own = pl.program_id(2)
    q = q_ref[...]
    km_hi, km_lo = _split2(km_ref[...])
    lane = lax.broadcasted_iota(jnp.int32, (1, LANES), 1)
    blk = lax.broadcasted_iota(jnp.int32, (nblk, tq), 0).astype(F32)
    zpad = jnp.zeros((half, tq), F32)
    zq = jnp.zeros((tq, LANES), BF16)
    q_plain = []
    q_aug = []
    for hh in range(2):
        mine = (lane // HEAD_DIM) == hh
        qh = jnp.where(mine, q, 0.0)
        q_hi, q_lo = _split2(qh)
        gate = _dot_nt(km_hi, q_hi) + _dot_nt(km_hi, q_lo) + _dot_nt(km_lo, q_hi)
        sel = _top3_rows(gate, blk < own.astype(F32), blk)
        bias = jnp.where(sel > 0.0, 0.0, NEG_INF)
        if nblk < half:
            bias = jnp.concatenate([bias, jnp.zeros((half - nblk, tq), F32)], axis=0)
        bias_t = (jnp.concatenate([zpad, bias], axis=0) if hh == 0 else jnp.concatenate([bias, zpad], axis=0)).T
        qpl = qh * (HEAD_DIM ** -0.5 * LOG2E)
        q_plain.append(qpl.astype(BF16))
        q_aug.append(jnp.where(mine, qpl, bias_t).astype(BF16))
    blockdiag = lambda a, b: jnp.concatenate(
        [jnp.concatenate([a, zq], axis=1), jnp.concatenate([zq, b], axis=1)], axis=0)
    q2_aug = blockdiag(q_aug[0], q_aug[1])

    def pv(vt2, p):
        return jnp.concatenate([_dot(vt2[:VAUG_ROWS], p[:, :tq]), _dot(vt2[VAUG_ROWS:], p[:, tq:])], axis=1)

    kpos = lax.broadcasted_iota(jnp.int32, (tq, 2 * tq), 0)
    qpos = lax.broadcasted_iota(jnp.int32, (tq, 2 * tq), 1) % tq
    s = jnp.where(kpos <= qpos, _dot_nt(kown_ref[...], blockdiag(q_plain[0], q_plain[1])), NEG_INF)
    m0 = jnp.max(s, axis=0, keepdims=True).astype(BF16).astype(F32)
    acc0 = pv(vown_ref[...], jnp.exp2(s - m0).astype(BF16))

    rows_g = KV_GROUP * MOBA_BLOCK
    last_g = nblk // KV_GROUP - 1

    def body(it, carry):
        m, acc = carry
        gs = [jnp.minimum(GROUPS_PER_TRIP * it + r, last_g) for r in range(GROUPS_PER_TRIP)]
        ss = [_dot_nt(k_ref[pl.ds(pl.multiple_of(g * rows_g, rows_g), rows_g), :], q2_aug) for g in gs]
        for r, (s, g) in enumerate(zip(ss, gs)):
            if r:
                s = jnp.where(GROUPS_PER_TRIP * it + r <= last_g, s, NEG_INF)
            sb = s.astype(BF16)
            m_new = jnp.maximum(m, jnp.max(sb, axis=0, keepdims=True).astype(F32))
            p = jnp.exp2(sb - m_new.astype(BF16))
            acc = jnp.exp2(m - m_new) * acc + pv(vt_ref[g], p)
            m = m_new
        return m, acc

    ngroups = (own + KV_GROUP - 1) // KV_GROUP
    _, acc = lax.fori_loop(0, (ngroups + GROUPS_PER_TRIP - 1) // GROUPS_PER_TRIP, body, (m0, acc0))
    for hh in range(2):
        a_h = acc[:, hh * tq:(hh + 1) * tq]
        o_ref[hh * HEAD_DIM:(hh + 1) * HEAD_DIM, :] = a_h[0:HEAD_DIM, :] / a_h[HEAD_DIM:HEAD_DIM + 1, :]


def _moba_prompt(q, kaug, vaug_t, kmean):
    bsz, t, a = q.shape
    nblk = t // MOBA_BLOCK
    nheads = a // HEAD_DIM
    half = LANES // 2
    assert t % MOBA_BLOCK == 0 and a % LANES == 0 and 2 * HEAD_DIM == LANES
    assert nblk % SUBLANES == 0 and nblk <= half and nblk % KV_GROUP == 0
    return pl.pallas_call(
        functools.partial(_moba_prompt_kernel, nblk=nblk),
        grid=(bsz, nheads // 2, nblk),
        in_specs=[
            pl.BlockSpec((None, MOBA_BLOCK, LANES), lambda b, hp, i: (b, i, hp)),
            pl.BlockSpec((None, MOBA_BLOCK, 2 * LANES), lambda b, hp, i: (b, i, hp)),
            pl.BlockSpec((None, None, 2 * VAUG_ROWS, MOBA_BLOCK), lambda b, hp, i: (b, i // KV_GROUP, hp, i % KV_GROUP)),
            pl.BlockSpec((None, t, 2 * LANES), lambda b, hp, i: (b, 0, hp)),
            pl.BlockSpec((None, nblk // KV_GROUP, 2 * VAUG_ROWS, KV_GROUP * MOBA_BLOCK), lambda b, hp, i: (b, 0, hp, 0)),
            pl.BlockSpec((None, nblk, LANES), lambda b, hp, i: (b, 0, hp)),
        ],
        out_specs=pl.BlockSpec((None, LANES, MOBA_BLOCK), lambda b, hp, i: (b, hp, i)),
        out_shape=jax.ShapeDtypeStruct((bsz, a, t), F32),
        compiler_params=_params(("parallel", "parallel", "arbitrary")),
        name="moba_prompt",
    )(q, kaug, vaug_t, kaug, vaug_t, kmean)


SAMPLE_BLOCKS_PER_STEP = 8

def _moba_sample_kernel(pt_ref, qrep_ref, kn_ref, vn_ref, *refs, pps, tq, nheads):
    del pt_ref
    k_pages = refs[:pps]
    v_pages = refs[pps:2 * pps]
    o_ref = refs[2 * pps]
    g_sc, m_sc, l_sc, o_sc = refs[2 * pps + 1:]
    c = pl.program_id(1)
    rows = nheads * tq
    a = nheads * HEAD_DIM
    page = k_pages[0].shape[-1]
    per_blk = MOBA_BLOCK // page
    nblk_step = pps // per_blk
    row_h = lax.broadcasted_iota(jnp.int32, (rows, a), 0) // tq
    lane_h = lax.broadcasted_iota(jnp.int32, (rows, a), 1) // HEAD_DIM
    qx = jnp.where(row_h == lane_h, qrep_ref[...], 0.0)
    q_hi, q_lo = _split2(qx)
    qs = (qx * (HEAD_DIM ** -0.5)).astype(BF16)
    flat = lambda x: x[...].reshape(a, page)
    scores = []
    for jj in range(nblk_step):
        kts = [flat(x) for x in k_pages[jj * per_blk:(jj + 1) * per_blk]]
        scores.append(_dot(qs, jnp.concatenate(kts, axis=1).astype(BF16)))
        ks_hi, ks_lo = _split2(sum(kts))
        graw = _dot(q_hi, ks_hi) + _dot(q_lo, ks_hi) + _dot(q_hi, ks_lo)
        g_sc[c * nblk_step + jj] = jnp.sum(graw, axis=1, keepdims=True) * (1.0 / MOBA_BLOCK)
    probs = []
    for jj in range(nblk_step):
        s = scores[jj]
        m = jnp.max(s, axis=1, keepdims=True)
        p = jnp.exp(s - m)
        m_sc[c * nblk_step + jj] = m
        l_sc[c * nblk_step + jj] = jnp.sum(p, axis=1, keepdims=True)
        probs.append(p.astype(BF16))
    for jj in range(nblk_step):
        vt = jnp.concatenate([flat(x) for x in v_pages[jj * per_blk:(jj + 1) * per_blk]], axis=1).astype(BF16)
        o_sc[c * nblk_step + jj] = _dot_nt(probs[jj], vt)

    @pl.when(c == pl.num_programs(1) - 1)
    def _():
        gates = g_sc[...]
        blk3 = lax.broadcasted_iota(jnp.int32, gates.shape, 0).astype(F32)
        sel = _top3_rows(gates, blk3 >= 0.0, blk3) > 0.0
        m_all = m_sc[...]
        kn = kn_ref[...]
        vn = vn_ref[...]
        t_idx = lax.broadcasted_iota(jnp.int32, (rows, 1), 0) % tq
        s_own = []
        m_own = jnp.full((rows, 1), NEG_INF, F32)
        for kk in range(tq):
            sk = jnp.sum(qx * (HEAD_DIM ** -0.5) * kn[kk:kk + 1, :], axis=1, keepdims=True)
            sk = jnp.where(t_idx >= kk, sk, NEG_INF)
            s_own.append(sk)
            m_own = jnp.maximum(m_own, sk)
        m_fin = jnp.maximum(m_own, jnp.max(jnp.where(sel, m_all, NEG_INF), axis=0))
        w = jnp.exp(jnp.where(sel, m_all - m_fin, NEG_INF))
        l_fin = jnp.sum(w * l_sc[...], axis=0)
        o_fin = jnp.sum(w * o_sc[...], axis=0)
        for kk in range(tq):
            pk = jnp.exp(s_own[kk] - m_fin)
            l_fin = l_fin + pk
            o_fin = o_fin + pk * vn[kk:kk + 1, :]
        o_ref[...] = o_fin / l_fin


def _moba_sample(q, k_new, v_new, cache_k, cache_v, layer, page_table):
    bs, tq, a = q.shape
    page, nheads = cache_k.shape[2], cache_k.shape[3]
    n_pages = page_table.shape[1]
    past = n_pages * page
    assert cache_k.shape[4] == HEAD_DIM and a == nheads * HEAD_DIM and page % LANES == 0
    assert MOBA_BLOCK % page == 0 and past % MOBA_BLOCK == 0 and tq <= MOBA_BLOCK
    per_blk = MOBA_BLOCK // page
    nb = past // MOBA_BLOCK
    assert nb >= MOBA_TOPK
    pps = per_blk * SAMPLE_BLOCKS_PER_STEP
    while n_pages % pps:
        pps -= per_blk
    rows = nheads * tq
    qrep = jnp.tile(q, (1, nheads, 1))
    ck_t = cache_k.transpose(0, 1, 3, 4, 2)
    cv_t = cache_v.transpose(0, 1, 3, 4, 2)

    def page_spec(r):
        return pl.BlockSpec((None, None, nheads, HEAD_DIM, page),
                            lambda b, c, pt: (layer, pt[b, c * pps + r], 0, 0, 0))

    seq_spec = pl.BlockSpec((None, tq, a), lambda b, c, pt: (b, 0, 0))
    row_spec = pl.BlockSpec((None, rows, a), lambda b, c, pt: (b, 0, 0))
    o_full = pl.pallas_call(
        functools.partial(_moba_sample_kernel, pps=pps, tq=tq, nheads=nheads),
        grid_spec=pltpu.PrefetchScalarGridSpec(
            num_scalar_prefetch=1,
            grid=(bs, n_pages // pps),
            in_specs=[row_spec, seq_spec, seq_spec] + [page_spec(r) for r in range(pps)] * 2,
            out_specs=row_spec,
            scratch_shapes=[
                pltpu.VMEM((nb, rows, 1), F32),
                pltpu.VMEM((nb, rows, 1), F32),
                pltpu.VMEM((nb, rows, 1), F32),
                pltpu.VMEM((nb, rows, a), F32),
            ],
        ),
        out_shape=jax.ShapeDtypeStruct((bs, rows, a), F32),
        compiler_params=_params(("parallel", "arbitrary")),
        name="moba_sample",
    )(page_table, qrep, k_new, v_new, *([ck_t] * pps), *([cv_t] * pps))
    o5 = o_full.reshape(bs, nheads, tq, nheads, HEAD_DIM)
    hidx = jnp.arange(nheads)
    return o5[:, hidx, :, hidx, :].transpose(1, 2, 0, 3).reshape(bs, tq, a)


def _ssd_prompt_kernel(xbc_ref, dt_ref, cw_ref, cb_ref, dtb_ref, alog_ref, dsk_ref, y_ref, hout_ref,
                       xwin_sc, h_sc, *, d_inner, nheads):
    cl = SSD_CHUNK
    c = pl.program_id(1)

    @pl.when(c == 0)
    def _():
        xwin_sc[...] = jnp.zeros(xwin_sc.shape, F32)
        h_sc[...] = jnp.zeros(h_sc.shape, F32)

    cur = xbc_ref[...]
    window = jnp.concatenate([xwin_sc[...], cur], axis=0)
    acc = jnp.broadcast_to(cb_ref[...], cur.shape)
    for i in range(CONV_W):
        back = CONV_W - 1 - i
        shifted = cur if back == 0 else pltpu.roll(window, back, 0)[SUBLANES:, :]
        acc = acc + shifted * cw_ref[i:i + 1, :]
    xwin_sc[...] = cur[cl - SUBLANES:cl, :]
    xc = _silu(acc)
    gn = SSM_GROUPS * D_STATE
    xs = xc[:, :d_inner]
    bm = xc[:, d_inner:d_inner + gn]
    cm = xc[:, d_inner + gn:d_inner + 2 * gn]

    dt = _softplus(dt_ref[...] + dtb_ref[...])
    da = dt * (-jnp.exp(alog_ref[...]))
    li = lax.broadcasted_iota(jnp.int32, (cl, cl), 0)
    si = lax.broadcasted_iota(jnp.int32, (cl, cl), 1)
    causal = li >= si
    tril = jnp.where(causal, 1.0, 0.0).astype(BF16)
    d1, d2, d3 = _split3(da)
    acs = _dot(tril, d1) + _dot(tril, d2) + _dot(tril, d3)
    acs_t = acs.T
    dt_t = dt.T
    xs_t = xs.T
    lane = lax.broadcasted_iota(jnp.int32, (1, LANES), 1)
    lo_half = lane < SSM_HEAD_DIM
    hpg = nheads // SSM_GROUPS
    ys = []
    for hp in range(nheads // 2):
        g = (2 * hp) // hpg
        bg = bm[:, g * D_STATE:(g + 1) * D_STATE]
        cg = cm[:, g * D_STATE:(g + 1) * D_STATE].astype(BF16)
        cb = _dot_nt(cg, bg.astype(BF16))
        x2 = xs[:, hp * LANES:(hp + 1) * LANES]
        x2b = x2.astype(BF16)
        yd = []
        ecol = []
        for hh in range(2):
            h = 2 * hp + hh
            acol = acs[:, h:h + 1]
            diff = acol - acs_t[h:h + 1, :]
            lm = jnp.exp(jnp.where(causal, diff, NEG_INF))
            mh = cb * lm * dt_t[h:h + 1, :]
            yd.append(_dot(mh.astype(BF16), x2b))
            ecol.append(jnp.exp(acol))
            alast = acs[cl - 1:cl, h:h + 1]
            wcol = jnp.exp(alast - acol) * dt[:, h:h + 1]
            st = _dot(xs_t[h * SSM_HEAD_DIM:(h + 1) * SSM_HEAD_DIM, :].astype(BF16), (bg * wcol).astype(BF16))
            hprev = h_sc[h]
            if hh == 0:
                hprev0 = hprev
            else:
                hprev2 = jnp.concatenate([hprev0, hprev], axis=0).astype(BF16)
            h_sc[h] = jnp.exp(alast) * hprev + st
        yo = _dot_nt(cg, hprev2) * jnp.where(lo_half, ecol[0], ecol[1])
        dsk2 = jnp.where(lo_half, dsk_ref[:, 2 * hp:2 * hp + 1], dsk_ref[:, 2 * hp + 1:2 * hp + 2])
        ys.append(jnp.where(lo_half, yd[0], yd[1]) + yo + dsk2 * x2)
    y_ref[...] = jnp.concatenate(ys, axis=1)

    @pl.when(c == pl.num_programs(1) - 1)
    def _():
        hout_ref[...] = h_sc[...]


def _ssd_prompt(xbc, dtp, cw, cb, dtb, alog, dsk, d_inner, nheads):
    bsz, t, cdim = xbc.shape
    assert t % SSD_CHUNK == 0 and nheads % 2 == 0 and (nheads // SSM_GROUPS) % 2 == 0
    nc = t // SSD_CHUNK
    small = lambda shape: pl.BlockSpec(shape, lambda b, c: (0, 0))
    return pl.pallas_call(
        functools.partial(_ssd_prompt_kernel, d_inner=d_inner, nheads=nheads),
        grid=(bsz, nc),
        in_specs=[
            pl.BlockSpec((None, SSD_CHUNK, cdim), lambda b, c: (b, c, 0)),
            pl.BlockSpec((None, SSD_CHUNK, LANES), lambda b, c: (b, c, 0)),
            small(cw.shape), small(cb.shape), small(dtb.shape), small(alog.shape), small(dsk.shape),
        ],
        out_specs=[
            pl.BlockSpec((None, SSD_CHUNK, d_inner), lambda b, c: (b, c, 0)),
            pl.BlockSpec((None, nheads, SSM_HEAD_DIM, D_STATE), lambda b, c: (b, 0, 0, 0)),
        ],
        out_shape=[
            jax.ShapeDtypeStruct((bsz, t, d_inner), F32),
            jax.ShapeDtypeStruct((bsz, nheads, SSM_HEAD_DIM, D_STATE), F32),
        ],
        scratch_shapes=[
            pltpu.VMEM((SUBLANES, cdim), F32),
            pltpu.VMEM((nheads, SSM_HEAD_DIM, D_STATE), F32),
        ],
        compiler_params=_params(("parallel", "arbitrary")),
        name="ssd_prompt",
    )(xbc, dtp, cw, cb, dtb, alog, dsk)


def _ssd_sample_kernel(xf_ref, dtx_ref, h0_ref, cw_ref, cb_ref, dtbx_ref, alogx_ref, dskx_ref, y_ref, hout_ref,
                       rows_sc, cpad_sc, bpad_sc, *, t, d_inner, nheads):
    cdim = xf_ref.shape[1]
    acc = jnp.broadcast_to(cb_ref[...], (t, cdim))
    for i in range(CONV_W):
        acc = acc + xf_ref[pl.ds(i, t), :] * cw_ref[i:i + 1, :]
    xc = _silu(acc)
    gn = SSM_GROUPS * D_STATE
    xs = xc[:, :d_inner]
    bm = xc[:, d_inner:d_inner + gn]
    cm = xc[:, d_inner + gn:d_inner + 2 * gn]
    dt = _softplus(dtx_ref[...] + dtbx_ref[...])
    da = dt * (-jnp.exp(alogx_ref[...]))
    acs = [da[0:1, :]]
    for s in range(1, t):
        acs.append(acs[-1] + da[s:s + 1, :])
    lane = lax.broadcasted_iota(jnp.int32, (1, d_inner), 1)
    grp = lane // (d_inner // SSM_GROUPS)

    rows_sc[...] = jnp.zeros(rows_sc.shape, F32)
    for s in range(t):
        rows_sc[s:s + 1, :] = jnp.exp(acs[t - 1] - acs[s]) * dt[s:s + 1, :] * xs[s:s + 1, :]
    rows_sc[t:t + 1, :] = jnp.exp(acs[t - 1])
    cols = rows_sc[...].T
    cpad_sc[...] = jnp.zeros(cpad_sc.shape, F32)
    bpad_sc[...] = jnp.zeros(bpad_sc.shape, F32)
    for g in range(SSM_GROUPS):
        cpad_sc[g, 0:t, :] = cm[:, g * D_STATE:(g + 1) * D_STATE]
        bpad_sc[g, 0:t, :] = bm[:, g * D_STATE:(g + 1) * D_STATE]

    h0 = h0_ref[...].reshape(d_inner, D_STATE)
    half = d_inner // SSM_GROUPS
    yoff_t = []
    hnew = []
    for g in range(SSM_GROUPS):
        h0g = h0[g * half:(g + 1) * half, :]
        yoff_t.append(_dot_nt(h0g.astype(BF16), cpad_sc[g].astype(BF16)))
        colg = cols[g * half:(g + 1) * half, :]
        w_hi, w_lo = _split2(jnp.where(lax.broadcasted_iota(jnp.int32, colg.shape, 1) < t, colg, 0.0))
        b_hi, b_lo = _split2(bpad_sc[g])
        upd = _dot(w_hi, b_hi) + _dot(w_hi, b_lo) + _dot(w_lo, b_hi)
        hnew.append(h0g * colg[:, t:t + 1] + upd)
    hout_ref[...] = jnp.concatenate(hnew, axis=0).reshape(hout_ref.shape)
    yoff = jnp.concatenate(yoff_t, axis=0).T

    for tt in range(t):
        y = yoff[tt:tt + 1, :] * jnp.exp(acs[tt]) + dskx_ref[...] * xs[tt:tt + 1, :]
        for s in range(tt + 1):
            cbs = []
            for g in range(SSM_GROUPS):
                sl = slice(g * D_STATE, (g + 1) * D_STATE)
                cbs.append(jnp.sum(cm[tt:tt + 1, sl] * bm[s:s + 1, sl], axis=1, keepdims=True))
            cbx = jnp.where(grp == 0, cbs[0], cbs[1])
            y = y + cbx * jnp.exp(acs[tt] - acs[s]) * dt[s:s + 1, :] * xs[s:s + 1, :]
        y_ref[tt:tt + 1, :] = y


def _ssd_sample(xf, dtx, h0, cw, cb, dtbx, alogx, dskx, t, d_inner, nheads):
    bs, tf, cdim = xf.shape
    assert SSM_GROUPS == 2 and t + 1 <= LANES
    small = lambda shape: pl.BlockSpec(shape, lambda b: (0, 0))
    return pl.pallas_call(
        functools.partial(_ssd_sample_kernel, t=t, d_inner=d_inner, nheads=nheads),
        grid=(bs,),
        in_specs=[
            pl.BlockSpec((None, tf, cdim), lambda b: (b, 0, 0)),
            pl.BlockSpec((None, t, d_inner), lambda b: (b, 0, 0)),
            pl.BlockSpec((None, nheads, SSM_HEAD_DIM, D_STATE), lambda b: (b, 0, 0, 0)),
            small(cw.shape), small(cb.shape), small(dtbx.shape), small(alogx.shape), small(dskx.shape),
        ],
        out_specs=[
            pl.BlockSpec((None, t, d_inner), lambda b: (b, 0, 0)),
            pl.BlockSpec((None, nheads, SSM_HEAD_DIM, D_STATE), lambda b: (b, 0, 0, 0)),
        ],
        out_shape=[
            jax.ShapeDtypeStruct((bs, t, d_inner), F32),
            jax.ShapeDtypeStruct((bs, nheads, SSM_HEAD_DIM, D_STATE), F32),
        ],
        scratch_shapes=[
            pltpu.VMEM((LANES, d_inner), F32),
            pltpu.VMEM((SSM_GROUPS, LANES, D_STATE), F32),
            pltpu.VMEM((SSM_GROUPS, LANES, D_STATE), F32),
        ],
        compiler_params=_params(("parallel",)),
        name="ssd_sample",
    )(xf, dtx, h0, cw, cb, dtbx, alogx, dskx)


def _merge_kernel(o_ref, za_ref, y_ref, zm_ref, g_ref, x_ref, p_ref, wa_ref, ws_ref, wo_ref, wpg_ref, wp_ref,
                  snw_ref, pnw_ref, fnw_ref, out_ref, *, final, o_transposed):
    d = x_ref.shape[1]
    o = o_ref[...].T if o_transposed else o_ref[...]
    ya = _dot((o * _silu(za_ref[...])).astype(BF16), wa_ref[...])
    yz = y_ref[...] * _silu(zm_ref[...])
    gw = yz.shape[1] // SSM_GROUPS
    parts = []
    for g in range(SSM_GROUPS):
        seg = yz[:, g * gw:(g + 1) * gw]
        parts.append(seg * lax.rsqrt(jnp.mean(seg * seg, axis=-1, keepdims=True) + EPS))
    yn = jnp.concatenate(parts, axis=1) * snw_ref[...]
    ys = _dot(yn.astype(BF16), ws_ref[...])
    mix = _sigmoid(g_ref[:, :d]) * ya + _sigmoid(g_ref[:, d:]) * ys
    h1 = x_ref[...] + _dot(mix.astype(BF16), wo_ref[...])
    pg = _sigmoid(_dot(_rms(h1, pnw_ref[...]).astype(BF16), wpg_ref[...]))
    h2 = h1 + _dot(p_ref[...].astype(BF16), wp_ref[...]) * pg
    out_ref[...] = _rms(h2, fnw_ref[...]) if final else h2


def _merge(o, za, y, zm, gl, x, p, weights, norms, tm, final):
    n, d = x.shape
    assert n % tm == 0
    row = lambda arr: pl.BlockSpec((tm, arr.shape[1]), lambda i: (i, 0))
    whole = lambda arr: pl.BlockSpec(arr.shape, lambda i: (0, 0))
    o_transposed = o.ndim == 3
    if o_transposed:
        tiles_per_seq = o.shape[2] // tm
        assert o.shape[2] % tm == 0 and o.shape[0] * o.shape[2] == n
        o_spec = pl.BlockSpec((None, o.shape[1], tm), lambda i: (i // tiles_per_seq, 0, i % tiles_per_seq))
    else:
        o_spec = row(o)
    acts = [o, za, y, zm, gl, x, p]
    return pl.pallas_call(
        functools.partial(_merge_kernel, final=final, o_transposed=o_transposed),
        grid=(n // tm,),
        in_specs=[o_spec] + [row(a) for a in acts[1:]] + [whole(w) for w in weights] + [whole(w) for w in norms],
        out_specs=pl.BlockSpec((tm, d), lambda i: (i, 0)),
        out_shape=jax.ShapeDtypeStruct((n, d), F32),
        compiler_params=_params(("parallel",)),
        name="merge_out",
    )(*acts, *weights, *norms)


def _layer(h, p_l, q_start, past, conv_state, ssm_state, lw, final_norm_w, final):
    (norm_mix_w, w_in, conv_w, conv_b, dt_bias, a_log, d_skip, ssm_norm_w,
     w_attn_br, w_ssm_br, w_out, ple_norm_w, w_ple_gate, w_ple) = lw
    bsz, t, d = h.shape
    a = w_attn_br.shape[0]
    d_inner = w_ssm_br.shape[0]
    nheads = dt_bias.shape[0]
    cdim = conv_w.shape[1]
    n = bsz * t
    row = lambda v: v.reshape(1, -1).astype(F32)

    sizes = (a, a, a, a, d_inner, cdim, nheads, 2 * d)
    offs = [0]
    for s in sizes:
        offs.append(offs[-1] + s)
    cols = [w_in[:, offs[i]:offs[i + 1]] for i in range(len(sizes))]
    cols[6] = jnp.pad(cols[6], ((0, 0), (0, LANES - nheads)))
    widths = (a, a, a, a, d_inner, cdim, LANES, 2 * d)
    w_pad = jnp.concatenate(cols, axis=1).astype(BF16)

    cos, sin = _rope_tables(t, q_start)
    prompt = past is None
    if prompt:
        tm = MOBA_BLOCK
        tiles_per_seq = t // tm
        cos_map = lambda i: (i % tiles_per_seq, 0)
    else:
        tm = n if n <= 256 else 256
        assert tm % t == 0
        cos = jnp.tile(cos, (tm // t, 1))
        sin = jnp.tile(sin, (tm // t, 1))
        cos_map = lambda i: (0, 0)
    outs = _inproj(h.reshape(n, d), row(norm_mix_w), w_pad, cos, sin, widths, tm, cos_map, t if prompt else None)
    nh_attn = a // HEAD_DIM
    cw = conv_w.astype(F32)
    cb = row(conv_b)
    if prompt:
        q, k_t, v_t, kaug, vaug_t, kmean, za, zm, xbc, dtp, gl = outs
        o = _moba_prompt(q.reshape(bsz, t, a), kaug.reshape(bsz, t, 2 * a), vaug_t, kmean.reshape(bsz, t // tm, a))
        k_new = k_t.reshape(bsz, nh_attn, HEAD_DIM, t).transpose(0, 3, 1, 2)
        v_new = v_t.reshape(bsz, nh_attn, HEAD_DIM, t).transpose(0, 3, 1, 2)
        pad = lambda vec: jnp.pad(row(vec), ((0, 0), (0, LANES - nheads)))
        y_ssd, ssm_new = _ssd_prompt(xbc.reshape(bsz, t, cdim), dtp.reshape(bsz, t, LANES), cw, cb,
                                     pad(dt_bias), pad(a_log), pad(d_skip), d_inner, nheads)
        if t < CONV_W - 1:
            raise NotImplementedError("prompt shorter than the conv window")
        conv_new = xbc.reshape(bsz, t, cdim)[:, t - (CONV_W - 1):, :]
    else:
        q, k, v, za, zm, xbc, dtp, gl = outs
        q3, k3, v3 = (z.reshape(bsz, t, a) for z in (q, k, v))
        k_new = k3.reshape(bsz, t, nh_attn, HEAD_DIM)
        v_new = v3.reshape(bsz, t, nh_attn, HEAD_DIM)
        cache_k, cache_v, layer, page_table = past
        o = _moba_sample(q3, k3, v3, cache_k, cache_v, layer, page_table).reshape(n, a)
        xf = jnp.concatenate([conv_state.astype(F32), xbc.reshape(bsz, t, cdim)], axis=1)
        conv_new = xf[:, -(CONV_W - 1):, :]
        rep = lambda vec: jnp.repeat(row(vec), SSM_HEAD_DIM, axis=1)
        dtx = jnp.repeat(dtp.reshape(bsz, t, LANES)[:, :, :nheads], SSM_HEAD_DIM, axis=2)
        y_ssd, ssm_new = _ssd_sample(xf, dtx, ssm_state.astype(F32), cw, cb, rep(dt_bias), rep(a_log),
                                     rep(d_skip), t, d_inner, nheads)

    weights = [w.astype(BF16) for w in (w_attn_br, w_ssm_br, w_out, w_ple_gate, w_ple)]
    norms = [row(ssm_norm_w), row(ple_norm_w), row(final_norm_w)]
    tm_out = 512 if (n % 512 == 0 and n > 512) else (256 if n % 256 == 0 else n)
    h_out = _merge(o, za, y_ssd.reshape(n, d_inner), zm, gl, h.reshape(n, d),
                   p_l.reshape(n, -1).astype(F32), weights, norms, tm_out, final)
    return h_out.reshape(bsz, t, d), k_new, v_new, conv_new, ssm_new


def kernel(x_prompt, x_sample, cache_k, cache_v, state_conv, state_ssm, page_table, p_prompt, p_sample,
           norm_mix_w, w_in, conv_w, conv_b, dt_bias, a_log, d_skip, ssm_norm_w,
           w_attn_br, w_ssm_br, w_out, ple_norm_w, w_ple_gate, w_ple, final_norm_w):
    depth = w_in.shape[0]
    past_len = page_table.shape[1] * cache_k.shape[2]
    h_p, h_s = x_prompt, x_sample
    acc = [[] for _ in range(8)]
    for l in range(depth):
        lw = (norm_mix_w[l], w_in[l], conv_w[l], conv_b[l], dt_bias[l], a_log[l], d_skip[l], ssm_norm_w[l],
              w_attn_br[l], w_ssm_br[l], w_out[l], ple_norm_w[l], w_ple_gate[l], w_ple[l])
        final = l == depth - 1
        h_p, kp, vp, cp, sp = _layer(h_p, p_prompt[l], 0, None, None, None, lw, final_norm_w, final)
        h_s, ks, vs, cs, ss = _layer(h_s, p_sample[l], past_len, (cache_k, cache_v, l, page_table),
                                     state_conv[l], state_ssm[l], lw, final_norm_w, final)
        for lst, val in zip(acc, (kp, vp, cp, sp, ks, vs, cs, ss)):
            lst.append(val)
    return (h_p, h_s) + tuple(jnp.stack(lst) for lst in acc)
```

```python
import functools

import jax
import jax.numpy as jnp
from jax import lax
from jax.experimental import pallas as pl
from jax.experimental.pallas import tpu as pltpu

F32 = jnp.float32
BF16 = jnp.bfloat16

HEAD_DIM = 64
MOBA_BLOCK = 256
MOBA_TOPK = 3
ROPE_THETA = 10000.0
SSM_HEAD_DIM = 64
SSM_GROUPS = 2
D_STATE = 128
CONV_W = 4
SSD_CHUNK = 128
EPS = 1e-6
NEG_INF = -1e30

LANES = 128
SUBLANES = 8
VMEM_LIMIT_BYTES = 56 * 1024 * 1024


def _sigmoid(x):
    return 0.5 * jnp.tanh(0.5 * x) + 0.5


def _silu(x):
    return x * _sigmoid(x)


def _softplus(x):
    return jnp.maximum(x, 0.0) + jnp.log1p(jnp.exp(-jnp.abs(x)))


def _rms(x, w):
    ms = jnp.mean(x * x, axis=-1, keepdims=True)
    return x * lax.rsqrt(ms + EPS) * w


def _dot(a, b):
    return jnp.dot(a, b, preferred_element_type=F32)


def _dot_nt(a, b):
    return lax.dot_general(a, b, (((1,), (1,)), ((), ())), preferred_element_type=F32)


def _split2(x):
    hi = x.astype(BF16)
    lo = (x - hi.astype(F32)).astype(BF16)
    return hi, lo


def _split3(x):
    b1 = x.astype(BF16)
    r1 = x - b1.astype(F32)
    b2 = r1.astype(BF16)
    b3 = (r1 - b2.astype(F32)).astype(BF16)
    return b1, b2, b3


def _params(sem):
    return pltpu.CompilerParams(dimension_semantics=sem, vmem_limit_bytes=VMEM_LIMIT_BYTES)


def _rope_table_kernel(inv_ref, cos_ref, sin_ref, *, start, rows):
    i = pl.program_id(0)
    pos = lax.broadcasted_iota(jnp.int32, (rows, LANES), 0) + (start + i * rows)
    lane = lax.broadcasted_iota(jnp.int32, (rows, LANES), 1)
    ang = pos.astype(F32) * inv_ref[...]
    first_half = (lane % HEAD_DIM) < (HEAD_DIM // 2)
    cos_ref[...] = jnp.cos(ang)
    s = jnp.sin(ang)
    sin_ref[...] = jnp.where(first_half, -s, s)


def _rope_tables(t, start):
    half = HEAD_DIM // 2
    inv = ROPE_THETA ** (-jnp.arange(half, dtype=F32) / half)
    inv = jnp.tile(inv, LANES // half)[None, :]
    rows = min(t, 512)
    assert t % rows == 0
    spec = pl.BlockSpec((rows, LANES), lambda i: (i, 0))
    return pl.pallas_call(
        functools.partial(_rope_table_kernel, start=start, rows=rows),
        grid=(t // rows,),
        in_specs=[pl.BlockSpec((1, LANES), lambda i: (0, 0))],
        out_specs=[spec, spec],
        out_shape=[jax.ShapeDtypeStruct((t, LANES), F32)] * 2,
        compiler_params=_params(("parallel",)),
        name="rope_tables",
    )(inv)


def _rope(x, cos, sin_signed):
    lane = lax.broadcasted_iota(jnp.int32, cos.shape, 1)
    first_half = (lane % HEAD_DIM) < (HEAD_DIM // 2)
    half = HEAD_DIM // 2
    outs = []
    for c in range(x.shape[1] // LANES):
        xs = x[:, c * LANES:(c + 1) * LANES]
        partner = jnp.where(first_half, pltpu.roll(xs, LANES - half, 1), pltpu.roll(xs, half, 1))
        outs.append(xs * cos + partner * sin_signed)
    return jnp.concatenate(outs, axis=1)


KV_GROUP = 2
GROUPS_PER_TRIP = 4
VAUG_ROWS = 80


def _inproj_kernel(x_ref, nw_ref, w_ref, cos_ref, sin_ref, *out_refs, widths, prompt, tiles_per_seq):
    u = _rms(x_ref[...], nw_ref[...]).astype(BF16)
    cos = cos_ref[...]
    sin = sin_ref[...]
    offs = [0]
    for wd in widths:
        offs.append(offs[-1] + wd)
    proj = lambda idx: _dot(u, w_ref[:, offs[idx]:offs[idx + 1]])
    q = _rope(proj(0), cos, sin)
    k = _rope(proj(1), cos, sin)
    v = proj(2)
    if not prompt:
        dense = dict(enumerate(out_refs))
        dense[0][...] = q
        dense[1][...] = k
        dense[2][...] = v
        first_dense = 3
    else:
        q_ref, kt_ref, vt_ref, kaug_ref, vaug_ref, kmean_ref = out_refs[:6]
        dense = {3 + j: r for j, r in enumerate(out_refs[6:])}
        first_dense = 3
        tm = q.shape[0]
        half = LANES // 2
        blk = pl.program_id(0) % tiles_per_seq
        q_ref[...] = q
        kmean_ref[...] = jnp.mean(k, axis=0, keepdims=True)
        kt_ref[...] = k.T
        v_t = v.T
        vt_ref[...] = v_t
        lane = lax.broadcasted_iota(jnp.int32, (1, LANES), 1)
        oh_hi = jnp.where(lane == half + blk, 1.0, 0.0)
        oh_lo = jnp.where(lane == blk, 1.0, 0.0)
        for p in range(k.shape[1] // LANES):
            k2 = k[:, p * LANES:(p + 1) * LANES]
            left = jnp.where(lane < half, k2, oh_hi)
            right = jnp.where(lane >= half, k2, oh_lo)
            kaug_ref[:, 2 * p * LANES:(2 * p + 2) * LANES] = jnp.concatenate([left, right], axis=1).astype(BF16)
        tail_row = lax.broadcasted_iota(jnp.int32, (VAUG_ROWS - HEAD_DIM, tm), 0)
        tail = jnp.where(tail_row == 0, 1.0, 0.0).astype(BF16)
        for h in range(v.shape[1] // HEAD_DIM):
            vh = v_t[h * HEAD_DIM:(h + 1) * HEAD_DIM, :].astype(BF16)
            vaug_ref[h * VAUG_ROWS:(h + 1) * VAUG_ROWS, :] = jnp.concatenate([vh, tail], axis=0)
    for idx in range(first_dense, len(widths)):
        dense[idx][...] = proj(idx)


def _inproj(x2d, nw, w_pad, cos, sin, widths, tm, cos_map, seq_len):
    n, d = x2d.shape
    assert n % tm == 0
    prompt = seq_len is not None
    rows = lambda wd: (jax.ShapeDtypeStruct((n, wd), F32), pl.BlockSpec((tm, wd), lambda i: (i, 0)))
    outs = [rows(wd) for wd in widths]
    tiles_per_seq = 1
    if prompt:
        assert tm == MOBA_BLOCK and seq_len % (KV_GROUP * tm) == 0 and seq_len // tm <= LANES // 2
        tiles_per_seq = seq_len // tm
        bsz = n // seq_len
        a = widths[1]
        nheads = a // HEAD_DIM
        tmap = lambda i: (i // tiles_per_seq, 0, i % tiles_per_seq)
        outs = [
            rows(widths[0]),
            (jax.ShapeDtypeStruct((bsz, a, seq_len), F32), pl.BlockSpec((None, a, tm), tmap)),
            (jax.ShapeDtypeStruct((bsz, a, seq_len), F32), pl.BlockSpec((None, a, tm), tmap)),
            (jax.ShapeDtypeStruct((n, 2 * a), BF16), pl.BlockSpec((tm, 2 * a), lambda i: (i, 0))),
            (jax.ShapeDtypeStruct((bsz, tiles_per_seq // KV_GROUP, nheads * VAUG_ROWS, KV_GROUP * tm), BF16),
             pl.BlockSpec((None, None, nheads * VAUG_ROWS, tm),
                          lambda i: (i // tiles_per_seq, (i % tiles_per_seq) // KV_GROUP, 0, i % KV_GROUP))),
            (jax.ShapeDtypeStruct((n // tm, 1, a), F32), pl.BlockSpec((None, 1, a), lambda i: (i, 0, 0))),
        ] + outs[3:]
    return pl.pallas_call(
        functools.partial(_inproj_kernel, widths=widths, prompt=prompt, tiles_per_seq=tiles_per_seq),
        grid=(n // tm,),
        in_specs=[
            pl.BlockSpec((tm, d), lambda i: (i, 0)),
            pl.BlockSpec((1, d), lambda i: (0, 0)),
            pl.BlockSpec(w_pad.shape, lambda i: (0, 0)),
            pl.BlockSpec((tm, LANES), cos_map),
            pl.BlockSpec((tm, LANES), cos_map),
        ],
        out_specs=[o[1] for o in outs],
        out_shape=[o[0] for o in outs],
        compiler_params=_params(("parallel",)),
        name="inproj",
    )(x2d, nw, w_pad, cos, sin)


LOG2E = 1.4426950408889634


def _top3_rows(gate, valid, blk):
    g = jnp.where(valid, gate, NEG_INF)
    sel = jnp.zeros(g.shape, F32)
    big = jnp.float32(1e9)
    for _ in range(MOBA_TOPK):
        mx = jnp.max(g, axis=0, keepdims=True)
        idx = jnp.min(jnp.where(g == mx, blk, big), axis=0, keepdims=True)
        pick = blk == idx
        sel = jnp.where(pick, 1.0, sel)
        g = jnp.where(pick, -jnp.inf, g)
    return jnp.where(valid, sel, 0.0)


def _moba_prompt_kernel(q_ref, kown_ref, vown_ref, k_ref, vt_ref, km_ref, o_ref, *, nblk):
    tq = MOBA_BLOCK
    half = LANES // 2
    own = pl.program_id(2)
    q = q_ref[...]
    km_hi, km_lo = _split2(km_ref[...])
    lane = lax.broadcasted_iota(jnp.int32, (1, LANES), 1)
    blk = lax.broadcasted_iota(jnp.int32, (nblk, tq), 0).astype(F32)
    zpad = jnp.zeros((half, tq), F32)
    zq = jnp.zeros((tq, LANES), BF16)
    q_plain = []
    q_aug = []
    for hh in range(2):
        mine = (lane // HEAD_DIM) == hh
        qh = jnp.where(mine, q, 0.0)
        q_hi, q_lo = _split2(qh)
        gate = _dot_nt(km_hi, q_hi) + _dot_nt(km_hi, q_lo) + _dot_nt(km_lo, q_hi)
        sel = _top3_rows(gate, blk < own.astype(F32), blk)
        bias = jnp.where(sel > 0.0, 0.0, NEG_INF)
        if nblk < half:
            bias = jnp.concatenate([bias, jnp.zeros((half - nblk, tq), F32)], axis=0)
        bias_t = (jnp.concatenate([zpad, bias], axis=0) if hh == 0 else jnp.concatenate([bias, zpad], axis=0)).T
        qpl = qh * (HEAD_DIM ** -0.5 * LOG2E)
        q_plain.append(qpl.astype(BF16))
        q_aug.append(jnp.where(mine, qpl, bias_t).astype(BF16))
    blockdiag = lambda a, b: jnp.concatenate(
        [jnp.concatenate([a, zq], axis=1), jnp.concatenate([zq, b], axis=1)], axis=0)
    q2_aug = blockdiag(q_aug[0], q_aug[1])

    def pv(vt2, p):
        return jnp.concatenate([_dot(vt2[:VAUG_ROWS], p[:, :tq]), _dot(vt2[VAUG_ROWS:], p[:, tq:])], axis=1)

    kpos = lax.broadcasted_iota(jnp.int32, (tq, 2 * tq), 0)
    qpos = lax.broadcasted_iota(jnp.int32, (tq, 2 * tq), 1) % tq
    s = jnp.where(kpos <= qpos, _dot_nt(kown_ref[...], blockdiag(q_plain[0], q_plain[1])), NEG_INF)
    m0 = jnp.max(s, axis=0, keepdims=True).astype(BF16).astype(F32)
    acc0 = pv(vown_ref[...], jnp.exp2(s - m0).astype(BF16))

    rows_g = KV_GROUP * MOBA_BLOCK
    last_g = nblk // KV_GROUP - 1

    def body(it, carry):
        m, acc = carry
        gs = [jnp.minimum(GROUPS_PER_TRIP * it + r, last_g) for r in range(GROUPS_PER_TRIP)]
        ss = [_dot_nt(k_ref[pl.ds(pl.multiple_of(g * rows_g, rows_g), rows_g), :], q2_aug) for g in gs]
        for r, (s, g) in enumerate(zip(ss, gs)):
            if r:
                s = jnp.where(GROUPS_PER_TRIP * it + r <= last_g, s, NEG_INF)
            sb = s.astype(BF16)
            m_new = jnp.maximum(m, jnp.max(sb, axis=0, keepdims=True).astype(F32))
            p = jnp.exp2(sb - m_new.astype(BF16))
            acc = jnp.exp2(m - m_new) * acc + pv(vt_ref[g], p)
            m = m_new
        return m, acc

    ngroups = (own + KV_GROUP - 1) // KV_GROUP
    _, acc = lax.fori_loop(0, (ngroups + GROUPS_PER_TRIP - 1) // GROUPS_PER_TRIP, body, (m0, acc0))
    for hh in range(2):
        a_h = acc[:, hh * tq:(hh + 1) * tq]
        o_ref[hh * HEAD_DIM:(hh + 1) * HEAD_DIM, :] = a_h[0:HEAD_DIM, :] / a_h[HEAD_DIM:HEAD_DIM + 1, :]


def _moba_prompt(q, kaug, vaug_t, kmean):
    bsz, t, a = q.shape
    nblk = t // MOBA_BLOCK
    nheads = a // HEAD_DIM
    half = LANES // 2
    assert t % MOBA_BLOCK == 0 and a % LANES == 0 and 2 * HEAD_DIM == LANES
    assert nblk % SUBLANES == 0 and nblk <= half and nblk % KV_GROUP == 0
    return pl.pallas_call(
        functools.partial(_moba_prompt_kernel, nblk=nblk),
        grid=(bsz, nheads // 2, nblk),
        in_specs=[
            pl.BlockSpec((None, MOBA_BLOCK, LANES), lambda b, hp, i: (b, i, hp)),
            pl.BlockSpec((None, MOBA_BLOCK, 2 * LANES), lambda b, hp, i: (b, i, hp)),
            pl.BlockSpec((None, None, 2 * VAUG_ROWS, MOBA_BLOCK), lambda b, hp, i: (b, i // KV_GROUP, hp, i % KV_GROUP)),
            pl.BlockSpec((None, t, 2 * LANES), lambda b, hp, i: (b, 0, hp)),
            pl.BlockSpec((None, nblk // KV_GROUP, 2 * VAUG_ROWS, KV_GROUP * MOBA_BLOCK), lambda b, hp, i: (b, 0, hp, 0)),
            pl.BlockSpec((None, nblk, LANES), lambda b, hp, i: (b, 0, hp)),
        ],
        out_specs=pl.BlockSpec((None, LANES, MOBA_BLOCK), lambda b, hp, i: (b, hp, i)),
        out_shape=jax.ShapeDtypeStruct((bsz, a, t), F32),
        compiler_params=_params(("parallel", "parallel", "arbitrary")),
        name="moba_prompt",
    )(q, kaug, vaug_t, kaug, vaug_t, kmean)


SAMPLE_BLOCKS_PER_STEP = 16

def _moba_sample_kernel(pt_ref, qrep_ref, kn_ref, vn_ref, *refs, pps, tq, nheads):
    del pt_ref
    k_pages = refs[:pps]
    v_pages = refs[pps:2 * pps]
    o_ref = refs[2 * pps]
    g_sc, m_sc, l_sc, o_sc = refs[2 * pps + 1:]
    c = pl.program_id(1)
    rows = nheads * tq
    a = nheads * HEAD_DIM
    page = k_pages[0].shape[-1]
    per_blk = MOBA_BLOCK // page
    nblk_step = pps // per_blk
    row_h = lax.broadcasted_iota(jnp.int32, (rows, a), 0) // tq
    lane_h = lax.broadcasted_iota(jnp.int32, (rows, a), 1) // HEAD_DIM
    qx = jnp.where(row_h == lane_h, qrep_ref[...], 0.0)
    q_hi, q_lo = _split2(qx)
    qs = (qx * (HEAD_DIM ** -0.5)).astype(BF16)
    flat = lambda x: x[...].reshape(a, page)
    scores = []
    for jj in range(nblk_step):
        kts = [flat(x) for x in k_pages[jj * per_blk:(jj + 1) * per_blk]]
        scores.append(_dot(qs, jnp.concatenate(kts, axis=1).astype(BF16)))
        ks_hi, ks_lo = _split2(sum(kts))
        graw = _dot(q_hi, ks_hi) + _dot(q_lo, ks_hi) + _dot(q_hi, ks_lo)
        g_sc[c * nblk_step + jj] = jnp.sum(graw, axis=1, keepdims=True) * (1.0 / MOBA_BLOCK)
    probs = []
    for jj in range(nblk_step):
        s = scores[jj]
        m = jnp.max(s, axis=1, keepdims=True)
        p = jnp.exp(s - m)
        m_sc[c * nblk_step + jj] = m
        l_sc[c * nblk_step + jj] = jnp.sum(p, axis=1, keepdims=True)
        probs.append(p.astype(BF16))
    for jj in range(nblk_step):
        vt = jnp.concatenate([flat(x) for x in v_pages[jj * per_blk:(jj + 1) * per_blk]], axis=1).astype(BF16)
        o_sc[c * nblk_step + jj] = _dot_nt(probs[jj], vt)

    @pl.when(c == pl.num_programs(1) - 1)
    def _():
        gates = g_sc[...]
        blk3 = lax.broadcasted_iota(jnp.int32, gates.shape, 0).astype(F32)
        sel = _top3_rows(gates, blk3 >= 0.0, blk3) > 0.0
        m_all = m_sc[...]
        kn = kn_ref[...]
        vn = vn_ref[...]
        t_idx = lax.broadcasted_iota(jnp.int32, (rows, 1), 0) % tq
        s_own = []
        m_own = jnp.full((rows, 1), NEG_INF, F32)
        for kk in range(tq):
            sk = jnp.sum(qx * (HEAD_DIM ** -0.5) * kn[kk:kk + 1, :], axis=1, keepdims=True)
            sk = jnp.where(t_idx >= kk, sk, NEG_INF)
            s_own.append(sk)
            m_own = jnp.maximum(m_own, sk)
        m_fin = jnp.maximum(m_own, jnp.max(jnp.where(sel, m_all, NEG_INF), axis=0))
        w = jnp.exp(jnp.where(sel, m_all - m_fin, NEG_INF))
        l_fin = jnp.sum(w * l_sc[...], axis=0)
        o_fin = jnp.sum(w * o_sc[...], axis=0)
        for kk in range(tq):
            pk = jnp.exp(s_own[kk] - m_fin)
            l_fin = l_fin + pk
            o_fin = o_fin + pk * vn[kk:kk + 1, :]
        o_ref[...] = o_fin / l_fin


def _moba_sample(q, k_new, v_new, cache_k, cache_v, layer, page_table):
    bs, tq, a = q.shape
    page, nheads = cache_k.shape[2], cache_k.shape[3]
    n_pages = page_table.shape[1]
    past = n_pages * page
    assert cache_k.shape[4] == HEAD_DIM and a == nheads * HEAD_DIM and page % LANES == 0
    assert MOBA_BLOCK % page == 0 and past % MOBA_BLOCK == 0 and tq <= MOBA_BLOCK
    per_blk = MOBA_BLOCK // page
    nb = past // MOBA_BLOCK
    assert nb >= MOBA_TOPK
    pps = per_blk * SAMPLE_BLOCKS_PER_STEP
    while n_pages % pps:
        pps -= per_blk
    rows = nheads * tq
    qrep = jnp.tile(q, (1, nheads, 1))
    ck_t = cache_k.transpose(0, 1, 3, 4, 2)
    cv_t = cache_v.transpose(0, 1, 3, 4, 2)

    def page_spec(r):
        return pl.BlockSpec((None, None, nheads, HEAD_DIM, page),
                            lambda b, c, pt: (layer, pt[b, c * pps + r], 0, 0, 0))

    seq_spec = pl.BlockSpec((None, tq, a), lambda b, c, pt: (b, 0, 0))
    row_spec = pl.BlockSpec((None, rows, a), lambda b, c, pt: (b, 0, 0))
    o_full = pl.pallas_call(
        functools.partial(_moba_sample_kernel, pps=pps, tq=tq, nheads=nheads),
        grid_spec=pltpu.PrefetchScalarGridSpec(
            num_scalar_prefetch=1,
            grid=(bs, n_pages // pps),
            in_specs=[row_spec, seq_spec, seq_spec] + [page_spec(r) for r in range(pps)] * 2,
            out_specs=row_spec,
            scratch_shapes=[
                pltpu.VMEM((nb, rows, 1), F32),
                pltpu.VMEM((nb, rows, 1), F32),
                pltpu.VMEM((nb, rows, 1), F32),
                pltpu.VMEM((nb, rows, a), F32),
            ],
        ),
        out_shape=jax.ShapeDtypeStruct((bs, rows, a), F32),
        compiler_params=_params(("parallel", "arbitrary")),
        name="moba_sample",
    )(page_table, qrep, k_new, v_new, *([ck_t] * pps), *([cv_t] * pps))
    o5 = o_full.reshape(bs, nheads, tq, nheads, HEAD_DIM)
    hidx = jnp.arange(nheads)
    return o5[:, hidx, :, hidx, :].transpose(1, 2, 0, 3).reshape(bs, tq, a)


def _ssd_prompt_kernel(xbc_ref, dt_ref, cw_ref, cb_ref, dtb_ref, alog_ref, dsk_ref, y_ref, hout_ref,
                       xwin_sc, h_sc, *, d_inner, nheads):
    cl = SSD_CHUNK
    c = pl.program_id(1)

    @pl.when(c == 0)
    def _():
        xwin_sc[...] = jnp.zeros(xwin_sc.shape, F32)
        h_sc[...] = jnp.zeros(h_sc.shape, F32)

    cur = xbc_ref[...]
    window = jnp.concatenate([xwin_sc[...], cur], axis=0)
    acc = jnp.broadcast_to(cb_ref[...], cur.shape)
    for i in range(CONV_W):
        back = CONV_W - 1 - i
        shifted = cur if back == 0 else pltpu.roll(window, back, 0)[SUBLANES:, :]
        acc = acc + shifted * cw_ref[i:i + 1, :]
    xwin_sc[...] = cur[cl - SUBLANES:cl, :]
    xc = _silu(acc)
    gn = SSM_GROUPS * D_STATE
    xs = xc[:, :d_inner]
    bm = xc[:, d_inner:d_inner + gn]
    cm = xc[:, d_inner + gn:d_inner + 2 * gn]

    dt = _softplus(dt_ref[...] + dtb_ref[...])
    da = dt * (-jnp.exp(alog_ref[...]))
    li = lax.broadcasted_iota(jnp.int32, (cl, cl), 0)
    si = lax.broadcasted_iota(jnp.int32, (cl, cl), 1)
    causal = li >= si
    tril = jnp.where(causal, 1.0, 0.0).astype(BF16)
    d1, d2, d3 = _split3(da)
    acs = _dot(tril, d1) + _dot(tril, d2) + _dot(tril, d3)
    acs_t = acs.T
    dt_t = dt.T
    xs_t = xs.T
    lane = lax.broadcasted_iota(jnp.int32, (1, LANES), 1)
    lo_half = lane < SSM_HEAD_DIM
    hpg = nheads // SSM_GROUPS
    ys = []
    for hp in range(nheads // 2):
        g = (2 * hp) // hpg
        bg = bm[:, g * D_STATE:(g + 1) * D_STATE]
        cg = cm[:, g * D_STATE:(g + 1) * D_STATE].astype(BF16)
        cb = _dot_nt(cg, bg.astype(BF16))
        x2 = xs[:, hp * LANES:(hp + 1) * LANES]
        x2b = x2.astype(BF16)
        yd = []
        ecol = []
        for hh in range(2):
            h = 2 * hp + hh
            acol = acs[:, h:h + 1]
            diff = acol - acs_t[h:h + 1, :]
            lm = jnp.exp(jnp.where(causal, diff, NEG_INF))
            mh = cb * lm * dt_t[h:h + 1, :]
            yd.append(_dot(mh.astype(BF16), x2b))
            ecol.append(jnp.exp(acol))
            alast = acs[cl - 1:cl, h:h + 1]
            wcol = jnp.exp(alast - acol) * dt[:, h:h + 1]
            st = _dot(xs_t[h * SSM_HEAD_DIM:(h + 1) * SSM_HEAD_DIM, :].astype(BF16), (bg * wcol).astype(BF16))
            hprev = h_sc[h]
            if hh == 0:
                hprev0 = hprev
            else:
                hprev2 = jnp.concatenate([hprev0, hprev], axis=0).astype(BF16)
            h_sc[h] = jnp.exp(alast) * hprev + st
        yo = _dot_nt(cg, hprev2) * jnp.where(lo_half, ecol[0], ecol[1])
        dsk2 = jnp.where(lo_half, dsk_ref[:, 2 * hp:2 * hp + 1], dsk_ref[:, 2 * hp + 1:2 * hp + 2])
        ys.append(jnp.where(lo_half, yd[0], yd[1]) + yo + dsk2 * x2)
    y_ref[...] = jnp.concatenate(ys, axis=1)

    @pl.when(c == pl.num_programs(1) - 1)
    def _():
        hout_ref[...] = h_sc[...]


def _ssd_prompt(xbc, dtp, cw, cb, dtb, alog, dsk, d_inner, nheads):
    bsz, t, cdim = xbc.shape
    assert t % SSD_CHUNK == 0 and nheads % 2 == 0 and (nheads // SSM_GROUPS) % 2 == 0
    nc = t // SSD_CHUNK
    small = lambda shape: pl.BlockSpec(shape, lambda b, c: (0, 0))
    return pl.pallas_call(
        functools.partial(_ssd_prompt_kernel, d_inner=d_inner, nheads=nheads),
        grid=(bsz, nc),
        in_specs=[
            pl.BlockSpec((None, SSD_CHUNK, cdim), lambda b, c: (b, c, 0)),
            pl.BlockSpec((None, SSD_CHUNK, LANES), lambda b, c: (b, c, 0)),
            small(cw.shape), small(cb.shape), small(dtb.shape), small(alog.shape), small(dsk.shape),
        ],
        out_specs=[
            pl.BlockSpec((None, SSD_CHUNK, d_inner), lambda b, c: (b, c, 0)),
            pl.BlockSpec((None, nheads, SSM_HEAD_DIM, D_STATE), lambda b, c: (b, 0, 0, 0)),
        ],
        out_shape=[
            jax.ShapeDtypeStruct((bsz, t, d_inner), F32),
            jax.ShapeDtypeStruct((bsz, nheads, SSM_HEAD_DIM, D_STATE), F32),
        ],
        scratch_shapes=[
            pltpu.VMEM((SUBLANES, cdim), F32),
            pltpu.VMEM((nheads, SSM_HEAD_DIM, D_STATE), F32),
        ],
        compiler_params=_params(("parallel", "arbitrary")),
        name="ssd_prompt",
    )(xbc, dtp, cw, cb, dtb, alog, dsk)


def _ssd_sample_kernel(xf_ref, dtx_ref, h0_ref, cw_ref, cb_ref, dtbx_ref, alogx_ref, dskx_ref, y_ref, hout_ref,
                       rows_sc, cpad_sc, bpad_sc, *, t, d_inner, nheads):
    cdim = xf_ref.shape[1]
    acc = jnp.broadcast_to(cb_ref[...], (t, cdim))
    for i in range(CONV_W):
        acc = acc + xf_ref[pl.ds(i, t), :] * cw_ref[i:i + 1, :]
    xc = _silu(acc)
    gn = SSM_GROUPS * D_STATE
    xs = xc[:, :d_inner]
    bm = xc[:, d_inner:d_inner + gn]
    cm = xc[:, d_inner + gn:d_inner + 2 * gn]
    dt = _softplus(dtx_ref[...] + dtbx_ref[...])
    da = dt * (-jnp.exp(alogx_ref[...]))
    acs = [da[0:1, :]]
    for s in range(1, t):
        acs.append(acs[-1] + da[s:s + 1, :])
    lane = lax.broadcasted_iota(jnp.int32, (1, d_inner), 1)
    grp = lane // (d_inner // SSM_GROUPS)

    rows_sc[...] = jnp.zeros(rows_sc.shape, F32)
    for s in range(t):
        rows_sc[s:s + 1, :] = jnp.exp(acs[t - 1] - acs[s]) * dt[s:s + 1, :] * xs[s:s + 1, :]
    rows_sc[t:t + 1, :] = jnp.exp(acs[t - 1])
    cols = rows_sc[...].T
    cpad_sc[...] = jnp.zeros(cpad_sc.shape, F32)
    bpad_sc[...] = jnp.zeros(bpad_sc.shape, F32)
    for g in range(SSM_GROUPS):
        cpad_sc[g, 0:t, :] = cm[:, g * D_STATE:(g + 1) * D_STATE]
        bpad_sc[g, 0:t, :] = bm[:, g * D_STATE:(g + 1) * D_STATE]

    h0 = h0_ref[...].reshape(d_inner, D_STATE)
    half = d_inner // SSM_GROUPS
    yoff_t = []
    hnew = []
    for g in range(SSM_GROUPS):
        h0g = h0[g * half:(g + 1) * half, :]
        yoff_t.append(_dot_nt(h0g.astype(BF16), cpad_sc[g].astype(BF16)))
        colg = cols[g * half:(g + 1) * half, :]
        w_hi, w_lo = _split2(jnp.where(lax.broadcasted_iota(jnp.int32, colg.shape, 1) < t, colg, 0.0))
        b_hi, b_lo = _split2(bpad_sc[g])
        upd = _dot(w_hi, b_hi) + _dot(w_hi, b_lo) + _dot(w_lo, b_hi)
        hnew.append(h0g * colg[:, t:t + 1] + upd)
    hout_ref[...] = jnp.concatenate(hnew, axis=0).reshape(hout_ref.shape)
    yoff = jnp.concatenate(yoff_t, axis=0).T

    for tt in range(t):
        y = yoff[tt:tt + 1, :] * jnp.exp(acs[tt]) + dskx_ref[...] * xs[tt:tt + 1, :]
        for s in range(tt + 1):
            cbs = []
            for g in range(SSM_GROUPS):
                sl = slice(g * D_STATE, (g + 1) * D_STATE)
                cbs.append(jnp.sum(cm[tt:tt + 1, sl] * bm[s:s + 1, sl], axis=1, keepdims=True))
            cbx = jnp.where(grp == 0, cbs[0], cbs[1])
            y = y + cbx * jnp.exp(acs[tt] - acs[s]) * dt[s:s + 1, :] * xs[s:s + 1, :]
        y_ref[tt:tt + 1, :] = y


def _ssd_sample(xf, dtx, h0, cw, cb, dtbx, alogx, dskx, t, d_inner, nheads):
    bs, tf, cdim = xf.shape
    assert SSM_GROUPS == 2 and t + 1 <= LANES
    small = lambda shape: pl.BlockSpec(shape, lambda b: (0, 0))
    return pl.pallas_call(
        functools.partial(_ssd_sample_kernel, t=t, d_inner=d_inner, nheads=nheads),
        grid=(bs,),
        in_specs=[
            pl.BlockSpec((None, tf, cdim), lambda b: (b, 0, 0)),
            pl.BlockSpec((None, t, d_inner), lambda b: (b, 0, 0)),
            pl.BlockSpec((None, nheads, SSM_HEAD_DIM, D_STATE), lambda b: (b, 0, 0, 0)),
            small(cw.shape), small(cb.shape), small(dtbx.shape), small(alogx.shape), small(dskx.shape),
        ],
        out_specs=[
            pl.BlockSpec((None, t, d_inner), lambda b: (b, 0, 0)),
            pl.BlockSpec((None, nheads, SSM_HEAD_DIM, D_STATE), lambda b: (b, 0, 0, 0)),
        ],
        out_shape=[
            jax.ShapeDtypeStruct((bs, t, d_inner), F32),
            jax.ShapeDtypeStruct((bs, nheads, SSM_HEAD_DIM, D_STATE), F32),
        ],
        scratch_shapes=[
            pltpu.VMEM((LANES, d_inner), F32),
            pltpu.VMEM((SSM_GROUPS, LANES, D_STATE), F32),
            pltpu.VMEM((SSM_GROUPS, LANES, D_STATE), F32),
        ],
        compiler_params=_params(("parallel",)),
        name="ssd_sample",
    )(xf, dtx, h0, cw, cb, dtbx, alogx, dskx)


def _merge_kernel(o_ref, za_ref, y_ref, zm_ref, g_ref, x_ref, p_ref, wa_ref, ws_ref, wo_ref, wpg_ref, wp_ref,
                  snw_ref, pnw_ref, fnw_ref, out_ref, *, final, o_transposed):
    d = x_ref.shape[1]
    o = o_ref[...].T if o_transposed else o_ref[...]
    ya = _dot((o * _silu(za_ref[...])).astype(BF16), wa_ref[...])
    yz = y_ref[...] * _silu(zm_ref[...])
    gw = yz.shape[1] // SSM_GROUPS
    parts = []
    for g in range(SSM_GROUPS):
        seg = yz[:, g * gw:(g + 1) * gw]
        parts.append(seg * lax.rsqrt(jnp.mean(seg * seg, axis=-1, keepdims=True) + EPS))
    yn = jnp.concatenate(parts, axis=1) * snw_ref[...]
    ys = _dot(yn.astype(BF16), ws_ref[...])
    mix = _sigmoid(g_ref[:, :d]) * ya + _sigmoid(g_ref[:, d:]) * ys
    h1 = x_ref[...] + _dot(mix.astype(BF16), wo_ref[...])
    pg = _sigmoid(_dot(_rms(h1, pnw_ref[...]).astype(BF16), wpg_ref[...]))
    h2 = h1 + _dot(p_ref[...].astype(BF16), wp_ref[...]) * pg
    out_ref[...] = _rms(h2, fnw_ref[...]) if final else h2


def _merge(o, za, y, zm, gl, x, p, weights, norms, tm, final):
    n, d = x.shape
    assert n % tm == 0
    row = lambda arr: pl.BlockSpec((tm, arr.shape[1]), lambda i: (i, 0))
    whole = lambda arr: pl.BlockSpec(arr.shape, lambda i: (0, 0))
    o_transposed = o.ndim == 3
    if o_transposed:
        tiles_per_seq = o.shape[2] // tm
        assert o.shape[2] % tm == 0 and o.shape[0] * o.shape[2] == n
        o_spec = pl.BlockSpec((None, o.shape[1], tm), lambda i: (i // tiles_per_seq, 0, i % tiles_per_seq))
    else:
        o_spec = row(o)
    acts = [o, za, y, zm, gl, x, p]
    return pl.pallas_call(
        functools.partial(_merge_kernel, final=final, o_transposed=o_transposed),
        grid=(n // tm,),
        in_specs=[o_spec] + [row(a) for a in acts[1:]] + [whole(w) for w in weights] + [whole(w) for w in norms],
        out_specs=pl.BlockSpec((tm, d), lambda i: (i, 0)),
        out_shape=jax.ShapeDtypeStruct((n, d), F32),
        compiler_params=_params(("parallel",)),
        name="merge_out",
    )(*acts, *weights, *norms)


def _layer(h, p_l, q_start, past, conv_state, ssm_state, lw, final_norm_w, final):
    (norm_mix_w, w_in, conv_w, conv_b, dt_bias, a_log, d_skip, ssm_norm_w,
     w_attn_br, w_ssm_br, w_out, ple_norm_w, w_ple_gate, w_ple) = lw
    bsz, t, d = h.shape
    a = w_attn_br.shape[0]
    d_inner = w_ssm_br.shape[0]
    nheads = dt_bias.shape[0]
    cdim = conv_w.shape[1]
    n = bsz * t
    row = lambda v: v.reshape(1, -1).astype(F32)

    sizes = (a, a, a, a, d_inner, cdim, nheads, 2 * d)
    offs = [0]
    for s in sizes:
        offs.append(offs[-1] + s)
    cols = [w_in[:, offs[i]:offs[i + 1]] for i in range(len(sizes))]
    cols[6] = jnp.pad(cols[6], ((0, 0), (0, LANES - nheads)))
    widths = (a, a, a, a, d_inner, cdim, LANES, 2 * d)
    w_pad = jnp.concatenate(cols, axis=1).astype(BF16)

    cos, sin = _rope_tables(t, q_start)
    prompt = past is None
    if prompt:
        tm = MOBA_BLOCK
        tiles_per_seq = t // tm
        cos_map = lambda i: (i % tiles_per_seq, 0)
    else:
        tm = n if n <= 256 else 256
        assert tm % t == 0
        cos = jnp.tile(cos, (tm // t, 1))
        sin = jnp.tile(sin, (tm // t, 1))
        cos_map = lambda i: (0, 0)
    outs = _inproj(h.reshape(n, d), row(norm_mix_w), w_pad, cos, sin, widths, tm, cos_map, t if prompt else None)
    nh_attn = a // HEAD_DIM
    cw = conv_w.astype(F32)
    cb = row(conv_b)
    if prompt:
        q, k_t, v_t, kaug, vaug_t, kmean, za, zm, xbc, dtp, gl = outs
        o = _moba_prompt(q.reshape(bsz, t, a), kaug.reshape(bsz, t, 2 * a), vaug_t, kmean.reshape(bsz, t // tm, a))
        k_new = k_t.reshape(bsz, nh_attn, HEAD_DIM, t).transpose(0, 3, 1, 2)
        v_new = v_t.reshape(bsz, nh_attn, HEAD_DIM, t).transpose(0, 3, 1, 2)
        pad = lambda vec: jnp.pad(row(vec), ((0, 0), (0, LANES - nheads)))
        y_ssd, ssm_new = _ssd_prompt(xbc.reshape(bsz, t, cdim), dtp.reshape(bsz, t, LANES), cw, cb,
                                     pad(dt_bias), pad(a_log), pad(d_skip), d_inner, nheads)
        if t < CONV_W - 1:
            raise NotImplementedError("prompt shorter than the conv window")
        conv_new = xbc.reshape(bsz, t, cdim)[:, t - (CONV_W - 1):, :]
    else:
        q, k, v, za, zm, xbc, dtp, gl = outs
        q3, k3, v3 = (z.reshape(bsz, t, a) for z in (q, k, v))
        k_new = k3.reshape(bsz, t, nh_attn, HEAD_DIM)
        v_new = v3.reshape(bsz, t, nh_attn, HEAD_DIM)
        cache_k, cache_v, layer, page_table = past
        o = _moba_sample(q3, k3, v3, cache_k, cache_v, layer, page_table).reshape(n, a)
        xf = jnp.concatenate([conv_state.astype(F32), xbc.reshape(bsz, t, cdim)], axis=1)
        conv_new = xf[:, -(CONV_W - 1):, :]
        rep = lambda vec: jnp.repeat(row(vec), SSM_HEAD_DIM, axis=1)
        dtx = jnp.repeat(dtp.reshape(bsz, t, LANES)[:, :, :nheads], SSM_HEAD_DIM, axis=2)
        y_ssd, ssm_new = _ssd_sample(xf, dtx, ssm_state.astype(F32), cw, cb, rep(dt_bias), rep(a_log),
                                     rep(d_skip), t, d_inner, nheads)

    weights = [w.astype(BF16) for w in (w_attn_br, w_ssm_br, w_out, w_ple_gate, w_ple)]
    norms = [row(ssm_norm_w), row(ple_norm_w), row(final_norm_w)]
    tm_out = 512 if (n % 512 == 0 and n > 512) else (256 if n % 256 == 0 else n)
    h_out = _merge(o, za, y_ssd.reshape(n, d_inner), zm, gl, h.reshape(n, d),
                   p_l.reshape(n, -1).astype(F32), weights, norms, tm_out, final)
    return h_out.reshape(bsz, t, d), k_new, v_new, conv_new, ssm_new


def kernel(x_prompt, x_sample, cache_k, cache_v, state_conv, state_ssm, page_table, p_prompt, p_sample,
           norm_mix_w, w_in, conv_w, conv_b, dt_bias, a_log, d_skip, ssm_norm_w,
           w_attn_br, w_ssm_br, w_out, ple_norm_w, w_ple_gate, w_ple, final_norm_w):
    depth = w_in.shape[0]
    past_len = page_table.shape[1] * cache_k.shape[2]
    h_p, h_s = x_prompt, x_sample
    acc = [[] for _ in range(8)]
    for l in range(depth):
        lw = (norm_mix_w[l], w_in[l], conv_w[l], conv_b[l], dt_bias[l], a_log[l], d_skip[l], ssm_norm_w[l],
              w_attn_br[l], w_ssm_br[l], w_out[l], ple_norm_w[l], w_ple_gate[l], w_ple[l])
        final = l == depth - 1
        h_p, kp, vp, cp, sp = _layer(h_p, p_prompt[l], 0, None, None, None, lw, final_norm_w, final)
        h_s, ks, vs, cs, ss = _layer(h_s, p_sample[l], past_len, (cache_k, cache_v, l, page_table),
                                     state_conv[l], state_ssm[l], lw, final_norm_w, final)
        for lst, val in zip(acc, (kp, vp, cp, sp, ks, vs, cs, ss)):
            lst.append(val)
    return (h_p, h_s) + tuple(jnp.stack(lst) for lst in acc)
```
